```python
import math
import jax
import jax.numpy as jnp
from jax import lax
import numpy as np

D_MODEL = 2048
BATCH = 4
SEQ = 4096
DEPTH = 2

N_EVEN = (DEPTH + 1) // 2
N_ODD = DEPTH // 2
D_FF = 5632
EPS = 1e-6
CONV_CH = D_MODEL // 2
CONV_WIDTH = 31
DN_HEADS = 8
DN_DK = 128
DN_DV = 128
DN_QK = DN_HEADS * DN_DK
DN_WIDTH = DN_HEADS * DN_DV
SHORT_CONV = 3
CHUNK = 64
EVEN_IN = 2 * CONV_CH + 2 * DN_QK + 2 * DN_WIDTH + 4 * DN_HEADS
DA_HEADS = 8
DA_DH = D_MODEL // (2 * DA_HEADS)
ROPE_THETA = 500000.0
ROPE_DIMS = DA_DH // 4
Q_BLOCK = 128

kernel_name = "hybrid_conformer_deltanet_diffattn_encoder"


def rmsnorm(x, g):
    xf = x.astype(jnp.float32)
    y = xf * lax.rsqrt(jnp.mean(xf * xf, axis=-1, keepdims=True) + EPS)
    return (y * g.astype(jnp.float32)).astype(x.dtype)


def layernorm(x, g, b):
    xf = x.astype(jnp.float32)
    mu = jnp.mean(xf, axis=-1, keepdims=True)
    xc = xf - mu
    y = xc * lax.rsqrt(jnp.mean(xc * xc, axis=-1, keepdims=True) + EPS)
    return (y * g.astype(jnp.float32) + b.astype(jnp.float32)).astype(x.dtype)


def l2norm(x):
    xf = x.astype(jnp.float32)
    return xf * lax.rsqrt(jnp.sum(xf * xf, axis=-1, keepdims=True) + EPS)


def swiglu(x, wg, wu, wd):
    return (jax.nn.silu(x @ wg) * (x @ wu)) @ wd


def depthwise_conv(x, w):
    k = w.shape[0]
    return lax.conv_general_dilated(
        x, w[:, None, :].astype(x.dtype), window_strides=(1,),
        padding=[(k // 2, k // 2)], dimension_numbers=("NWC", "WIO", "NWC"),
        feature_group_count=x.shape[-1])


def gated_delta_rule(q, k, v, g, beta):
    b, h, s, dk = q.shape
    dv = v.shape[-1]
    n = s // CHUNK
    q = q.reshape(b, h, n, CHUNK, dk)
    k = k.reshape(b, h, n, CHUNK, dk)
    v = v.reshape(b, h, n, CHUNK, dv)
    beta = beta.reshape(b, h, n, CHUNK)
    gc = jnp.cumsum(g.reshape(b, h, n, CHUNK), axis=-1)
    idx = jnp.arange(CHUNK)
    incl = idx[:, None] >= idx[None, :]
    strict = idx[:, None] > idx[None, :]
    decay = jnp.exp(jnp.where(incl, gc[..., :, None] - gc[..., None, :], -jnp.inf))
    kb = k * beta[..., None]
    vb = v * beta[..., None]
    lmat = jnp.where(strict, jnp.einsum("bhnid,bhnjd->bhnij", kb, k) * decay, 0.0)
    eye = jnp.eye(CHUNK, dtype=jnp.float32)
    tmat = lax.linalg.triangular_solve(lmat + eye, jnp.broadcast_to(eye, lmat.shape),
                                       left_side=True, lower=True, unit_diagonal=True)
    gexp = jnp.exp(gc)
    u = tmat @ vb
    w = tmat @ (kb * gexp[..., None])
    a_intra = jnp.einsum("bhnid,bhnjd->bhnij", q, k) * decay
    q_dec = q * gexp[..., None]
    g_last = gc[..., -1]
    k_dec = k * jnp.exp(g_last[..., None] - gc)[..., None]
    xs = tuple(jnp.moveaxis(t, 2, 0) for t in (u, w, q_dec, k_dec, a_intra, jnp.exp(g_last)))

    def step(state, inp):
        u_i, w_i, qd_i, kd_i, a_i, gl_i = inp
        v_new = u_i - jnp.einsum("bhcd,bhde->bhce", w_i, state)
        o_i = jnp.einsum("bhcd,bhde->bhce", qd_i, state) + jnp.einsum("bhij,bhje->bhie", a_i, v_new)
        state = state * gl_i[..., None, None] + jnp.einsum("bhcd,bhce->bhde", kd_i, v_new)
        return state, o_i

    state0 = jnp.zeros((b, h, dk, dv), jnp.float32)
    _, o = lax.scan(step, state0, xs)
    return jnp.moveaxis(o, 0, 2).reshape(b, h, s, dv)


def even_mixer(hx, w_in, conv_w, conv_b, ln_g, ln_b, short_w, a_log, dt_bias, onorm_g, w_out):
    b, s, _ = hx.shape
    p = hx @ w_in
    i1 = CONV_CH
    i2 = 2 * CONV_CH
    i3 = i2 + 2 * DN_QK + DN_WIDTH
    i4 = i3 + DN_WIDTH
    i5 = i4 + 2 * DN_HEADS
    glu_v, glu_g, qkv, z, beta_raw, alpha_raw = jnp.split(p, [i1, i2, i3, i4, i5], axis=-1)
    a = glu_v * jax.nn.sigmoid(glu_g)
    a = depthwise_conv(a, conv_w) + conv_b.astype(a.dtype)
    a = jax.nn.silu(layernorm(a, ln_g, ln_b))
    qkv = jax.nn.silu(depthwise_conv(qkv, short_w))
    q, k, v = jnp.split(qkv, [DN_QK, 2 * DN_QK], axis=-1)
    q = l2norm(q.reshape(b, s, DN_HEADS, DN_DK)) * (DN_DK ** -0.5)
    k = l2norm(k.reshape(b, s, DN_HEADS, DN_DK))
    v = v.reshape(b, s, DN_HEADS, DN_DV).astype(jnp.float32)
    beta = jax.nn.sigmoid(beta_raw.astype(jnp.float32)).reshape(b, s, 2, DN_HEADS)
    g = -jnp.exp(a_log.astype(jnp.float32)) * jax.nn.softplus(
        alpha_raw.astype(jnp.float32).reshape(b, s, 2, DN_HEADS) + dt_bias.astype(jnp.float32))
    qh = jnp.transpose(q, (0, 2, 1, 3))
    kh = jnp.transpose(k, (0, 2, 1, 3))
    vh = jnp.transpose(v, (0, 2, 1, 3))
    g_f = jnp.transpose(g[:, :, 0], (0, 2, 1))
    g_b = jnp.transpose(g[:, :, 1], (0, 2, 1))
    b_f = jnp.transpose(beta[:, :, 0], (0, 2, 1))
    b_b = jnp.transpose(beta[:, :, 1], (0, 2, 1))
    o_fwd = gated_delta_rule(qh, kh, vh, g_f, b_f)
    o_bwd = jnp.flip(gated_delta_rule(jnp.flip(qh, 2), jnp.flip(kh, 2), jnp.flip(vh, 2),
                                      jnp.flip(g_b, 2), jnp.flip(b_b, 2)), 2)
    o = jnp.transpose(o_fwd + o_bwd, (0, 2, 1, 3))
    o = rmsnorm(o, onorm_g) * jax.nn.silu(z.reshape(b, s, DN_HEADS, DN_DV).astype(jnp.float32))
    o = o.reshape(b, s, DN_WIDTH).astype(hx.dtype)
    return jnp.concatenate([a, o], axis=-1) @ w_out


def partial_rotary(x, cos, sin):
    half = ROPE_DIMS // 2
    x1 = x[..., :half]
    x2 = x[..., half:ROPE_DIMS]
    return jnp.concatenate([x1 * cos - x2 * sin, x2 * cos + x1 * sin, x[..., ROPE_DIMS:]], axis=-1)


def odd_mixer(hx, positions, w_qkv, lq1, lk1, lq2, lk2, subln_g, w_o, lambda_init):
    b, s, _ = hx.shape
    q, k, v = jnp.split(hx @ w_qkv, [D_MODEL, 2 * D_MODEL], axis=-1)
    q = q.reshape(b, s, DA_HEADS, 2, DA_DH)
    k = k.reshape(b, s, DA_HEADS, 2, DA_DH)
    v = v.reshape(b, s, DA_HEADS, 2 * DA_DH)
    inv_freq = ROPE_THETA ** (-jnp.arange(0, ROPE_DIMS, 2, dtype=jnp.float32) / ROPE_DIMS)
    ang = positions.astype(jnp.float32)[..., None] * inv_freq
    cos = jnp.cos(ang)[:, :, None, None, :].astype(hx.dtype)
    sin = jnp.sin(ang)[:, :, None, None, :].astype(hx.dtype)
    q = partial_rotary(q, cos, sin) * (DA_DH ** -0.5)
    k = partial_rotary(k, cos, sin)
    nb = s // Q_BLOCK
    qb = jnp.transpose(q.reshape(b, nb, Q_BLOCK, DA_HEADS, 2, DA_DH), (1, 0, 3, 4, 2, 5))
    kt = jnp.transpose(k, (0, 2, 3, 1, 4))
    vt = jnp.transpose(v, (0, 2, 1, 3))
    lam = (jnp.exp(jnp.sum(lq1.astype(jnp.float32) * lk1.astype(jnp.float32)))
           - jnp.exp(jnp.sum(lq2.astype(jnp.float32) * lk2.astype(jnp.float32))) + lambda_init)

    def attend(q_blk):
        sc = jnp.einsum("bhmqd,bhmkd->bhmqk", q_blk, kt).astype(jnp.float32)
        pr = jax.nn.softmax(sc, axis=-1)
        wgt = pr[:, :, 0] - lam * pr[:, :, 1]
        return jnp.einsum("bhqk,bhke->bhqe", wgt.astype(vt.dtype), vt)

    o = lax.map(attend, qb)
    o = jnp.transpose(o, (1, 0, 3, 2, 4)).reshape(b, s, DA_HEADS, 2 * DA_DH)
    o = rmsnorm(o, subln_g) * (1.0 - lambda_init)
    return o.reshape(b, s, D_MODEL) @ w_o


def setup_inputs(seed: int = 0) -> dict:
    key = jax.random.key(seed)
    ks = iter(jax.random.split(key, 48))

    def nrm(shape, scale):
        return jax.random.normal(next(ks), shape, jnp.float32) * scale

    def gain(shape):
        return 1.0 + nrm(shape, 0.02)

    x = nrm((BATCH, SEQ, D_MODEL), 1.0)
    positions = jnp.broadcast_to(jnp.arange(SEQ, dtype=jnp.int32), (BATCH, SEQ))
    norm_ffn1 = gain((DEPTH, D_MODEL))
    ffn1_wg = nrm((DEPTH, D_MODEL, D_FF), D_MODEL ** -0.5)
    ffn1_wu = nrm((DEPTH, D_MODEL, D_FF), D_MODEL ** -0.5)
    ffn1_wd = nrm((DEPTH, D_FF, D_MODEL), D_FF ** -0.5)
    norm_mix = gain((DEPTH, D_MODEL))
    norm_ffn2 = gain((DEPTH, D_MODEL))
    ffn2_wg = nrm((DEPTH, D_MODEL, D_FF), D_MODEL ** -0.5)
    ffn2_wu = nrm((DEPTH, D_MODEL, D_FF), D_MODEL ** -0.5)
    ffn2_wd = nrm((DEPTH, D_FF, D_MODEL), D_FF ** -0.5)
    ev_w_in = nrm((N_EVEN, D_MODEL, EVEN_IN), D_MODEL ** -0.5)
    ev_conv_w = nrm((N_EVEN, CONV_WIDTH, CONV_CH), CONV_WIDTH ** -0.5)
    ev_conv_b = nrm((N_EVEN, CONV_CH), 0.02)
    ev_ln_g = gain((N_EVEN, CONV_CH))
    ev_ln_b = nrm((N_EVEN, CONV_CH), 0.02)
    ev_short_w = nrm((N_EVEN, SHORT_CONV, 2 * DN_QK + DN_WIDTH), SHORT_CONV ** -0.5)
    ev_a_log = jnp.log(jax.random.uniform(next(ks), (N_EVEN, 2, DN_HEADS), jnp.float32, 1.0, 16.0))
    dt = jnp.exp(jax.random.uniform(next(ks), (N_EVEN, 2, DN_HEADS), jnp.float32,
                                    math.log(1e-3), math.log(1e-1)))
    ev_dt_bias = dt + jnp.log(-jnp.expm1(-dt))
    ev_onorm_g = gain((N_EVEN, DN_DV))
    ev_w_out = nrm((N_EVEN, CONV_CH + DN_WIDTH, D_MODEL), (CONV_CH + DN_WIDTH) ** -0.5)
    od_w_qkv = nrm((N_ODD, D_MODEL, 3 * D_MODEL), D_MODEL ** -0.5)
    od_lq1 = nrm((N_ODD, DA_DH), 0.1)
    od_lk1 = nrm((N_ODD, DA_DH), 0.1)
    od_lq2 = nrm((N_ODD, DA_DH), 0.1)
    od_lk2 = nrm((N_ODD, DA_DH), 0.1)
    od_subln_g = gain((N_ODD, 2 * DA_DH))
    od_w_o = nrm((N_ODD, D_MODEL, D_MODEL), D_MODEL ** -0.5)
    final_norm = gain((D_MODEL,))
    return {"x": x, "positions": positions,
            "norm_ffn1": norm_ffn1, "ffn1_wg": ffn1_wg, "ffn1_wu": ffn1_wu, "ffn1_wd": ffn1_wd,
            "norm_mix": norm_mix,
            "norm_ffn2": norm_ffn2, "ffn2_wg": ffn2_wg, "ffn2_wu": ffn2_wu, "ffn2_wd": ffn2_wd,
            "ev_w_in": ev_w_in, "ev_conv_w": ev_conv_w, "ev_conv_b": ev_conv_b,
            "ev_ln_g": ev_ln_g, "ev_ln_b": ev_ln_b, "ev_short_w": ev_short_w,
            "ev_a_log": ev_a_log, "ev_dt_bias": ev_dt_bias, "ev_onorm_g": ev_onorm_g, "ev_w_out": ev_w_out,
            "od_w_qkv": od_w_qkv, "od_lq1": od_lq1, "od_lk1": od_lk1, "od_lq2": od_lq2, "od_lk2": od_lk2,
            "od_subln_g": od_subln_g, "od_w_o": od_w_o,
            "final_norm": final_norm}


def reference(x, positions, norm_ffn1, ffn1_wg, ffn1_wu, ffn1_wd, norm_mix,
              norm_ffn2, ffn2_wg, ffn2_wu, ffn2_wd,
              ev_w_in, ev_conv_w, ev_conv_b, ev_ln_g, ev_ln_b, ev_short_w,
              ev_a_log, ev_dt_bias, ev_onorm_g, ev_w_out,
              od_w_qkv, od_lq1, od_lk1, od_lq2, od_lk2, od_subln_g, od_w_o,
              final_norm):
    for i in range(DEPTH):
        x = x + 0.5 * swiglu(rmsnorm(x, norm_ffn1[i]), ffn1_wg[i], ffn1_wu[i], ffn1_wd[i])
        hx = rmsnorm(x, norm_mix[i])
        j = i // 2
        if i % 2 == 0:
            x = x + even_mixer(hx, ev_w_in[j], ev_conv_w[j], ev_conv_b[j], ev_ln_g[j], ev_ln_b[j],
                               ev_short_w[j], ev_a_log[j], ev_dt_bias[j], ev_onorm_g[j], ev_w_out[j])
        else:
            lambda_init = 0.8 - 0.6 * math.exp(-0.3 * i)
            x = x + odd_mixer(hx, positions, od_w_qkv[j], od_lq1[j], od_lk1[j], od_lq2[j], od_lk2[j],
                              od_subln_g[j], od_w_o[j], lambda_init)
        x = x + 0.5 * swiglu(rmsnorm(x, norm_ffn2[i]), ffn2_wg[i], ffn2_wu[i], ffn2_wd[i])
    return rmsnorm(x, final_norm)
```

```python
import functools
import math

import jax
import jax.numpy as jnp
from jax import lax
from jax.experimental import pallas as pl
from jax.experimental.pallas import tpu as pltpu

F32 = jnp.float32
BF16 = jnp.bfloat16
EPS = 1e-6

D_FF = 5632
CONV_CH = 1024
CONV_WIDTH = 31
DN_HEADS = 8
DN_DK = 128
DN_DV = 128
SHORT_CONV = 3
CHUNK = 64
DA_HEADS = 8
DA_DH = 128
ROPE_THETA = 500000.0
ROPE_DIMS = DA_DH // 4

LANES = 128
CONV_HALO = 16
SHORT_HALO = 8
EVEN_IN_PAD = 6272
SCAL_COL_BLOCK = 6144 // LANES
VMEM_LIMIT = 56 * 1024 * 1024
HIGHEST = lax.Precision.HIGHEST


def _sigmoid(x):
    return 1.0 / (1.0 + jnp.exp(-x))


def _silu(x):
    return x * _sigmoid(x)


def _softplus(x):
    return jnp.maximum(x, 0.0) + jnp.log(1.0 + jnp.exp(-jnp.abs(x)))


def _rms(x, g):
    ms = jnp.mean(x * x, axis=-1, keepdims=True)
    return x * lax.rsqrt(ms + EPS) * g


def _dot(a, b):
    return jnp.dot(a, b, preferred_element_type=F32)


def _dot_nt(a, b, precision=None):
    return lax.dot_general(a, b, (((1,), (1,)), ((), ())), precision=precision,
                           preferred_element_type=F32)


def _params(sem):
    return pltpu.CompilerParams(dimension_semantics=sem, vmem_limit_bytes=VMEM_LIMIT)


def _ffn_kernel(x_ref, g_ref, wg_ref, wu_ref, wd_ref, *rest, final):
    if final:
        fg_ref, o_ref, xn_ref, acc_ref = rest
    else:
        o_ref, xn_ref, acc_ref = rest
    j = pl.program_id(1)

    @pl.when(j == 0)
    def _():
        xn_ref[...] = _rms(x_ref[...], g_ref[...]).astype(BF16)
        acc_ref[...] = jnp.zeros_like(acc_ref)

    xn = xn_ref[...]
    a = _dot(xn, wg_ref[...])
    b = _dot(xn, wu_ref[...])
    h = (_silu(a) * b).astype(BF16)
    acc_ref[...] += _dot(h, wd_ref[...])

    @pl.when(j == pl.num_programs(1) - 1)
    def _():
        y = x_ref[...] + 0.5 * acc_ref[...]
        if final:
            y = _rms(y, fg_ref[...])
        o_ref[...] = y


def _ffn(x, g, wg, wu, wd, final_g=None):
    t, d = x.shape
    ff = wg.shape[1]
    tm = min(512, t)
    tf = 512
    final = final_g is not None
    in_specs = [
        pl.BlockSpec((tm, d), lambda i, j: (i, 0)),
        pl.BlockSpec((1, d), lambda i, j: (0, 0)),
        pl.BlockSpec((d, tf), lambda i, j: (0, j)),
        pl.BlockSpec((d, tf), lambda i, j: (0, j)),
        pl.BlockSpec((tf, d), lambda i, j: (j, 0)),
    ]
    args = [x, g.reshape(1, d), wg.astype(BF16), wu.astype(BF16), wd.astype(BF16)]
    if final:
        in_specs.append(pl.BlockSpec((1, d), lambda i, j: (0, 0)))
        args.append(final_g.reshape(1, d))
    return pl.pallas_call(
        functools.partial(_ffn_kernel, final=final),
        grid=(t // tm, ff // tf),
        in_specs=in_specs,
        out_specs=pl.BlockSpec((tm, d), lambda i, j: (i, 0)),
        out_shape=jax.ShapeDtypeStruct((t, d), F32),
        scratch_shapes=[pltpu.VMEM((tm, d), BF16), pltpu.VMEM((tm, d), F32)],
        compiler_params=_params(("parallel", "arbitrary")),
        name="ffn",
    )(*args)


def _norm_mm_kernel(x_ref, g_ref, w_ref, o_ref, xn_ref):
    @pl.when(pl.program_id(1) == 0)
    def _():
        xn_ref[...] = _rms(x_ref[...], g_ref[...]).astype(BF16)

    o_ref[...] = _dot(xn_ref[...], w_ref[...])


def _norm_matmul(x, g, w, tn):
    t, d = x.shape
    n = w.shape[1]
    tm = min(512, t)
    return pl.pallas_call(
        _norm_mm_kernel,
        grid=(t // tm, n // tn),
        in_specs=[
            pl.BlockSpec((tm, d), lambda i, j: (i, 0)),
            pl.BlockSpec((1, d), lambda i, j: (0, 0)),
            pl.BlockSpec((d, tn), lambda i, j: (0, j)),
        ],
        out_specs=pl.BlockSpec((tm, tn), lambda i, j: (i, j)),
        out_shape=jax.ShapeDtypeStruct((t, n), F32),
        scratch_shapes=[pltpu.VMEM((tm, d), BF16)],
        compiler_params=_params(("parallel", "arbitrary")),
        name="norm_matmul",
    )(x, g.reshape(1, d), w)


def _conv_kernel(pv, pg, hvp, hgp, hvn, hgn, w_ref, o_ref, scr, *, ts):
    i = pl.program_id(1)
    last = pl.num_programs(1) - 1
    prev = jnp.where(i > 0, hvp[0] * _sigmoid(hgp[0]), 0.0)
    nxt = jnp.where(i < last, hvn[0] * _sigmoid(hgn[0]), 0.0)
    scr[0:CONV_HALO, :] = prev
    scr[CONV_HALO:CONV_HALO + ts, :] = pv[0] * _sigmoid(pg[0])
    scr[CONV_HALO + ts:2 * CONV_HALO + ts, :] = nxt
    acc = w_ref[0:1, :] * scr[1:1 + ts, :]
    for k in range(1, CONV_WIDTH):
        acc = acc + w_ref[k:k + 1, :] * scr[1 + k:1 + k + ts, :]
    o_ref[0] = acc


def _glu_conv(p3, conv_w, ts):
    b, s, _ = p3.shape
    nc = CONV_CH // LANES
    r = ts // CONV_HALO
    nh = s // CONV_HALO

    def main(off):
        return pl.BlockSpec((1, ts, LANES), lambda bb, i, c: (bb, i, off + c))

    def prev(off):
        return pl.BlockSpec((1, CONV_HALO, LANES),
                            lambda bb, i, c: (bb, jnp.maximum(i * r - 1, 0), off + c))

    def nxt(off):
        return pl.BlockSpec((1, CONV_HALO, LANES),
                            lambda bb, i, c: (bb, jnp.minimum((i + 1) * r, nh - 1), off + c))

    return pl.pallas_call(
        functools.partial(_conv_kernel, ts=ts),
        grid=(b, s // ts, nc),
        in_specs=[main(0), main(nc), prev(0), prev(nc), nxt(0), nxt(nc),
                  pl.BlockSpec((CONV_WIDTH, LANES), lambda bb, i, c: (0, c))],
        out_specs=pl.BlockSpec((1, ts, LANES), lambda bb, i, c: (bb, i, c)),
        out_shape=jax.ShapeDtypeStruct((b, s, CONV_CH), F32),
        scratch_shapes=[pltpu.VMEM((ts + 2 * CONV_HALO, LANES), F32)],
        compiler_params=_params(("parallel", "parallel", "parallel")),
        name="glu_conv",
    )(p3, p3, p3, p3, p3, p3, conv_w)


def _short_conv_kernel(x_ref, hp, hn, w_ref, o_ref, scr, *, ts):
    i = pl.program_id(1)
    c = pl.program_id(2)
    last = pl.num_programs(1) - 1
    scr[0:SHORT_HALO, :] = jnp.where(i > 0, hp[0], 0.0)
    scr[SHORT_HALO:SHORT_HALO + ts, :] = x_ref[0]
    scr[SHORT_HALO + ts:2 * SHORT_HALO + ts, :] = jnp.where(i < last, hn[0], 0.0)
    y = (w_ref[0:1, :] * scr[SHORT_HALO - 1:SHORT_HALO - 1 + ts, :]
         + w_ref[1:2, :] * scr[SHORT_HALO:SHORT_HALO + ts, :]
         + w_ref[2:3, :] * scr[SHORT_HALO + 1:SHORT_HALO + 1 + ts, :])
    y = _silu(y)
    yn = y * lax.rsqrt(jnp.sum(y * y, axis=-1, keepdims=True) + EPS)
    o_ref[0] = jnp.where(c < DN_HEADS, yn * (DN_DK ** -0.5), jnp.where(c < 2 * DN_HEADS, yn, y))


def _short_conv(p3, short_w, ts):
    b, s, _ = p3.shape
    nc = short_w.shape[1] // LANES
    off = 2 * CONV_CH // LANES
    r = ts // SHORT_HALO
    nh = s // SHORT_HALO
    return pl.pallas_call(
        functools.partial(_short_conv_kernel, ts=ts),
        grid=(b, s // ts, nc),
        in_specs=[
            pl.BlockSpec((1, ts, LANES), lambda bb, i, c: (bb, i, off + c)),
            pl.BlockSpec((1, SHORT_HALO, LANES),
                         lambda bb, i, c: (bb, jnp.maximum(i * r - 1, 0), off + c)),
            pl.BlockSpec((1, SHORT_HALO, LANES),
                         lambda bb, i, c: (bb, jnp.minimum((i + 1) * r, nh - 1), off + c)),
            pl.BlockSpec((SHORT_CONV, LANES), lambda bb, i, c: (0, c)),
        ],
        out_specs=pl.BlockSpec((1, ts, LANES), lambda bb, i, c: (bb, i, c)),
        out_shape=jax.ShapeDtypeStruct((b, s, nc * LANES), F32),
        scratch_shapes=[pltpu.VMEM((ts + 2 * SHORT_HALO, LANES), F32)],
        compiler_params=_params(("parallel", "parallel", "parallel")),
        name="short_conv",
    )(p3, p3, p3, short_w)


def _delta_prep_kernel(q_ref, k_ref, v_ref, sc_ref, alog_ref, dtb_ref,
                       u_ref, wq_ref, akd_ref, gl_ref):
    sc = sc_ref[0]
    beta = _sigmoid(sc)
    araw = pltpu.roll(sc, LANES - 2 * DN_HEADS, 1)
    g = -jnp.exp(alog_ref[...]) * _softplus(araw + dtb_ref[...])

    ii = lax.broadcasted_iota(jnp.int32, (CHUNK, CHUNK), 0)
    jj = lax.broadcasted_iota(jnp.int32, (CHUNK, CHUNK), 1)
    low = (ii >= jj).astype(F32)
    upp = (ii <= jj).astype(F32)
    gcf = jnp.dot(low, g, precision=HIGHEST, preferred_element_type=F32)
    gcb = jnp.dot(upp, g, precision=HIGHEST, preferred_element_type=F32)
    lane = lax.broadcasted_iota(jnp.int32, (CHUNK, LANES), 1)
    gc = jnp.where(lane < DN_HEADS, gcf, gcb)
    gtot = jnp.broadcast_to(gcf[CHUNK - 1:CHUNK, :], (CHUNK, LANES))
    gexp = jnp.exp(gc)
    kdec = jnp.exp(gtot - gc)

    r128 = lax.broadcasted_iota(jnp.int32, (LANES, LANES), 0)
    c128 = lax.broadcasted_iota(jnp.int32, (LANES, LANES), 1)
    eye128 = (r128 == c128).astype(F32)
    gct = _dot_nt(eye128, gc, precision=HIGHEST)
    gtt = _dot_nt(eye128, gtot, precision=HIGHEST)

    eye64 = (ii == jj).astype(F32)
    eye128b = eye128.astype(BF16)
    for h in range(DN_HEADS):
        sl = slice(h * DN_DK, (h + 1) * DN_DK)
        qh = q_ref[0, :, sl]
        kh = k_ref[0, :, sl]
        vh = v_ref[0, :, sl]
        khb = kh.astype(BF16)
        kk = _dot_nt(khb, khb)
        qk = _dot_nt(qh.astype(BF16), khb)
        for d in range(2):
            c = d * DN_HEADS + h
            incl = (ii >= jj) if d == 0 else (ii <= jj)
            strict = (ii > jj) if d == 0 else (ii < jj)
            bcol = beta[:, c:c + 1]
            gcol = gc[:, c:c + 1]
            dec = jnp.exp(jnp.where(incl, gcol - gct[c:c + 1, :], -jnp.inf))
            m = jnp.where(strict, -(kk * bcol) * dec, 0.0)
            tinv = eye64 + m
            xp = m
            for _ in range(5):
                xb = xp.astype(BF16)
                xp = _dot(xb, xb)
                tinv = tinv + _dot(tinv.astype(BF16), xp.astype(BF16))
            gex = gexp[:, c:c + 1]
            rhs = jnp.concatenate([vh * bcol, kh * (bcol * gex)], axis=1).astype(BF16)
            uw = _dot(tinv.astype(BF16), rhs)
            a_in = jnp.where(incl, qk * dec, 0.0)
            kd = (kh * kdec[:, c:c + 1]).astype(BF16)
            kdt = _dot_nt(eye128b, kd)
            u_ref[0, 0, c] = uw[:, 0:DN_DV]
            wq_ref[0, 0, c] = jnp.concatenate([uw[:, DN_DV:], qh * gex], axis=0).astype(BF16)
            akd_ref[0, 0, c] = jnp.concatenate([a_in, kdt], axis=0).astype(BF16)
            glc = jnp.exp(gtt[c:c + 1, :])
            gl_ref[0, 0, c] = jnp.broadcast_to(jnp.concatenate([glc, glc], axis=1), (8, LANES))


def _delta_prep(qkvn, p3, a_log, dt_bias):
    b, s, _ = qkvn.shape
    n = s // CHUNK
    nch = 2 * DN_HEADS
    w = DN_HEADS * DN_DK

    def pad_row(a):
        return jnp.pad(a.reshape(1, nch), ((0, 0), (0, LANES - nch)))

    return pl.pallas_call(
        _delta_prep_kernel,
        grid=(b, n),
        in_specs=[
            pl.BlockSpec((1, CHUNK, w), lambda bb, i: (bb, i, 0)),
            pl.BlockSpec((1, CHUNK, w), lambda bb, i: (bb, i, 1)),
            pl.BlockSpec((1, CHUNK, w), lambda bb, i: (bb, i, 2)),
            pl.BlockSpec((1, CHUNK, LANES), lambda bb, i: (bb, i, SCAL_COL_BLOCK)),
            pl.BlockSpec((1, LANES), lambda bb, i: (0, 0)),
            pl.BlockSpec((1, LANES), lambda bb, i: (0, 0)),
        ],
        out_specs=[
            pl.BlockSpec((1, 1, nch, CHUNK, DN_DV), lambda bb, i: (bb, i, 0, 0, 0)),
            pl.BlockSpec((1, 1, nch, 2 * CHUNK, DN_DK), lambda bb, i: (bb, i, 0, 0, 0)),
            pl.BlockSpec((1, 1, nch, CHUNK + DN_DK, CHUNK), lambda bb, i: (bb, i, 0, 0, 0)),
            pl.BlockSpec((1, 1, nch, 8, LANES), lambda bb, i: (bb, i, 0, 0, 0)),
        ],
        out_shape=[
            jax.ShapeDtypeStruct((b, n, nch, CHUNK, DN_DV), F32),
            jax.ShapeDtypeStruct((b, n, nch, 2 * CHUNK, DN_DK), BF16),
            jax.ShapeDtypeStruct((b, n, nch, CHUNK + DN_DK, CHUNK), BF16),
            jax.ShapeDtypeStruct((b, n, nch, 8, LANES), F32),
        ],
        compiler_params=_params(("parallel", "parallel")),
        name="delta_prep",
    )(qkvn, qkvn, qkvn, p3, pad_row(a_log), pad_row(dt_bias))


def _delta_scan_kernel(uf, wqf, akdf, glf, ub, wqb, akdb, glb, of_ref, ob_ref, st_ref, *, hb):
    @pl.when(pl.program_id(2) == 0)
    def _():
        st_ref[...] = jnp.zeros_like(st_ref)

    for d, (u_r, wq_r, akd_r, gl_r, o_r) in enumerate(((uf, wqf, akdf, glf, of_ref),
                                                       (ub, wqb, akdb, glb, ob_ref))):
        for h in range(hb):
            ch = d * hb + h
            st = st_ref[ch]
            ws = _dot(wq_r[0, 0, h], st.astype(BF16))
            vnew = u_r[0, 0, h] - ws[0:CHUNK]
            r = _dot(akd_r[0, 0, h], vnew.astype(BF16))
            o_r[0, :, h * DN_DV:(h + 1) * DN_DV] = ws[CHUNK:] + r[0:CHUNK]
            st_ref[ch] = st * gl_r[0, 0, h, 0:1, :] + r[CHUNK:]


def _delta_scan(u, wq, akd, gl, hb=4):
    b, n = u.shape[0], u.shape[1]
    s = n * CHUNK
    ng = DN_HEADS // hb
    gb = ng

    def spec(shape, back):
        if back:
            return pl.BlockSpec(shape, lambda bb, g, i: (bb, n - 1 - i, gb + g) + (0,) * (len(shape) - 3))
        return pl.BlockSpec(shape, lambda bb, g, i: (bb, i, g) + (0,) * (len(shape) - 3))

    shapes = [(1, 1, hb, CHUNK, DN_DV), (1, 1, hb, 2 * CHUNK, DN_DK),
              (1, 1, hb, CHUNK + DN_DK, CHUNK), (1, 1, hb, 8, LANES)]
    in_specs = [spec(sh, False) for sh in shapes] + [spec(sh, True) for sh in shapes]
    ow = hb * DN_DV
    return pl.pallas_call(
        functools.partial(_delta_scan_kernel, hb=hb),
        grid=(b, ng, n),
        in_specs=in_specs,
        out_specs=[pl.BlockSpec((1, CHUNK, ow), lambda bb, g, i: (bb, i, g)),
                   pl.BlockSpec((1, CHUNK, ow), lambda bb, g, i: (bb, n - 1 - i, g))],
        out_shape=[jax.ShapeDtypeStruct((b, s, DN_HEADS * DN_DV), F32)] * 2,
        scratch_shapes=[pltpu.VMEM((2 * hb, DN_DK, DN_DV), F32)],
        compiler_params=_params(("parallel", "parallel", "arbitrary")),
        name="delta_scan",
    )(u, wq, akd, gl, u, wq, akd, gl)


def _even_out_kernel(cv_ref, cb_ref, lg_ref, lb_ref, of_ref, ob_ref, z_ref, og_ref, w_ref, x_ref,
                     o_ref, lhs_ref):
    @pl.when(pl.program_id(1) == 0)
    def _():
        a = cv_ref[...] + cb_ref[...]
        mu = jnp.mean(a, axis=-1, keepdims=True)
        xc = a - mu
        y = xc * lax.rsqrt(jnp.mean(xc * xc, axis=-1, keepdims=True) + EPS)
        lhs_ref[:, 0:CONV_CH] = _silu(y * lg_ref[...] + lb_ref[...]).astype(BF16)
        for h in range(DN_HEADS):
            sl = slice(h * DN_DV, (h + 1) * DN_DV)
            o = of_ref[:, sl] + ob_ref[:, sl]
            lhs_ref[:, CONV_CH + h * DN_DV:CONV_CH + (h + 1) * DN_DV] = (
                _rms(o, og_ref[...]) * _silu(z_ref[:, sl])).astype(BF16)

    o_ref[...] = x_ref[...] + _dot(lhs_ref[...], w_ref[...])


def _even_out(conv, conv_b, ln_g, ln_b, o_f, o_b, p, onorm_g, w_out, x):
    t, d = x.shape
    tm = min(512, t)
    tn = 1024
    kk = CONV_CH + DN_HEADS * DN_DV
    zblk = (2 * CONV_CH + 3 * DN_HEADS * DN_DK) // CONV_CH

    def row(n):
        return pl.BlockSpec((1, n), lambda i, j: (0, 0))

    def tile(n, jblk=0):
        return pl.BlockSpec((tm, n), lambda i, j: (i, jblk))

    return pl.pallas_call(
        _even_out_kernel,
        grid=(t // tm, d // tn),
        in_specs=[tile(CONV_CH), row(CONV_CH), row(CONV_CH), row(CONV_CH),
                  tile(CONV_CH), tile(CONV_CH), tile(CONV_CH, zblk), row(DN_DV),
                  pl.BlockSpec((kk, tn), lambda i, j: (0, j)),
                  pl.BlockSpec((tm, tn), lambda i, j: (i, j))],
        out_specs=pl.BlockSpec((tm, tn), lambda i, j: (i, j)),
        out_shape=jax.ShapeDtypeStruct((t, d), F32),
        scratch_shapes=[pltpu.VMEM((tm, kk), BF16)],
        compiler_params=_params(("parallel", "arbitrary")),
        name="even_out",
    )(conv, conv_b.reshape(1, -1), ln_g.reshape(1, -1), ln_b.reshape(1, -1), o_f, o_b, p,
      onorm_g.reshape(1, -1), w_out.astype(BF16), x)


def _even_mixer(x, b, s, norm_g, w_in, conv_w, conv_b, ln_g, ln_b, short_w, a_log, dt_bias,
                onorm_g, w_out):
    t = b * s
    w_pad = jnp.pad(w_in, ((0, 0), (0, EVEN_IN_PAD - w_in.shape[1]))).astype(BF16)
    p = _norm_matmul(x, norm_g, w_pad, tn=EVEN_IN_PAD // 7)
    p3 = p.reshape(b, s, EVEN_IN_PAD)
    ts = min(512, s)
    conv = _glu_conv(p3, conv_w, ts)
    qkvn = _short_conv(p3, short_w, ts)
    u, wq, akd, gl = _delta_prep(qkvn, p3, a_log, dt_bias)
    o_f, o_b = _delta_scan(u, wq, akd, gl)
    return _even_out(conv.reshape(t, -1), conv_b, ln_g, ln_b, o_f.reshape(t, -1), o_b.reshape(t, -1),
                     p, onorm_g, w_out, x)


def _rope_kernel(x_ref, pos_ref, freq_ref, o_ref, cs_ref):
    j = pl.program_id(1)
    nrep = x_ref.shape[1] // LANES

    @pl.when(j == 0)
    def _():
        ang = pos_ref[...].astype(F32) * freq_ref[...]
        lane = lax.broadcasted_iota(jnp.int32, ang.shape, 1)
        half = ROPE_DIMS // 2
        cs_ref[0] = jnp.where(lane < ROPE_DIMS, jnp.cos(ang), 1.0)
        sn = jnp.sin(ang)
        cs_ref[1] = jnp.where(lane < half, -sn, 0.0)
        cs_ref[2] = jnp.where((lane >= half) & (lane < ROPE_DIMS), sn, 0.0)

    x = x_ref[...]
    nq = DA_HEADS * 2 * DA_DH // x_ref.shape[1]

    @pl.when(j < 2 * nq)
    def _():
        half = ROPE_DIMS // 2
        w = x.shape[1]
        c = jnp.concatenate([cs_ref[0]] * nrep, axis=1)
        s1 = jnp.concatenate([cs_ref[1]] * nrep, axis=1)
        s2 = jnp.concatenate([cs_ref[2]] * nrep, axis=1)
        y = x * c + pltpu.roll(x, w - half, 1) * s1 + pltpu.roll(x, half, 1) * s2
        o_ref[...] = (y * jnp.where(j < nq, DA_DH ** -0.5, 1.0)).astype(BF16)

    @pl.when(j >= 2 * nq)
    def _():
        o_ref[...] = x.astype(BF16)


def _rope_cast(qkv, pos, freq):
    t, n = qkv.shape
    tm = min(512, t)
    tn = 1024
    return pl.pallas_call(
        _rope_kernel,
        grid=(t // tm, n // tn),
        in_specs=[pl.BlockSpec((tm, tn), lambda i, j: (i, j)),
                  pl.BlockSpec((tm, 1), lambda i, j: (i, 0)),
                  pl.BlockSpec((1, LANES), lambda i, j: (0, 0))],
        out_specs=pl.BlockSpec((tm, tn), lambda i, j: (i, j)),
        out_shape=jax.ShapeDtypeStruct((t, n), BF16),
        scratch_shapes=[pltpu.VMEM((3, tm, LANES), F32)],
        compiler_params=_params(("parallel", "arbitrary")),
        name="rope_cast",
    )(qkv, pos, freq)


def _attn_kernel(q_ref, k_ref, v_ref, lq1, lk1, lq2, lk2, sg_ref, o_ref, *, lambda_init):
    lam = (jnp.exp(jnp.sum(lq1[...] * lk1[...], axis=-1, keepdims=True))
           - jnp.exp(jnp.sum(lq2[...] * lk2[...], axis=-1, keepdims=True)) + lambda_init)
    q = q_ref[...]
    k = k_ref[...]
    s1 = _dot_nt(q[:, 0:DA_DH], k[:, 0:DA_DH])
    s2 = _dot_nt(q[:, DA_DH:], k[:, DA_DH:])
    e1 = jnp.exp(s1 - jnp.max(s1, axis=-1, keepdims=True))
    e2 = jnp.exp(s2 - jnp.max(s2, axis=-1, keepdims=True))
    r1 = 1.0 / jnp.sum(e1, axis=-1, keepdims=True)
    r2 = lam / jnp.sum(e2, axis=-1, keepdims=True)
    wgt = (e1 * r1 - e2 * r2).astype(BF16)
    o = _dot(wgt, v_ref[...])
    o_ref[...] = (_rms(o, sg_ref[...]) * (1.0 - lambda_init)).astype(BF16)


def _diff_attention(qkvb, b, s, lq1, lk1, lq2, lk2, subln_g, lambda_init):
    t = b * s
    hw = 2 * DA_DH
    tq = min(256, s)
    nq = s // tq

    def row(n):
        return pl.BlockSpec((1, n), lambda bb, h, i: (0, 0))

    return pl.pallas_call(
        functools.partial(_attn_kernel, lambda_init=lambda_init),
        grid=(b, DA_HEADS, nq),
        in_specs=[pl.BlockSpec((tq, hw), lambda bb, h, i: (bb * nq + i, h)),
                  pl.BlockSpec((s, hw), lambda bb, h, i: (bb, DA_HEADS + h)),
                  pl.BlockSpec((s, hw), lambda bb, h, i: (bb, 2 * DA_HEADS + h)),
                  row(DA_DH), row(DA_DH), row(DA_DH), row(DA_DH), row(hw)],
        out_specs=pl.BlockSpec((tq, hw), lambda bb, h, i: (bb * nq + i, h)),
        out_shape=jax.ShapeDtypeStruct((t, DA_HEADS * hw), BF16),
        compiler_params=_params(("parallel", "parallel", "arbitrary")),
        name="diff_attention",
    )(qkvb, qkvb, qkvb, lq1.reshape(1, -1), lk1.reshape(1, -1), lq2.reshape(1, -1),
      lk2.reshape(1, -1), subln_g.reshape(1, -1))


def _mm_res_kernel(a_ref, w_ref, x_ref, o_ref):
    o_ref[...] = x_ref[...] + _dot(a_ref[...], w_ref[...])


def _matmul_residual(a, w, x):
    t, d = x.shape
    kk = a.shape[1]
    tm = min(512, t)
    tn = 1024
    return pl.pallas_call(
        _mm_res_kernel,
        grid=(t // tm, d // tn),
        in_specs=[pl.BlockSpec((tm, kk), lambda i, j: (i, 0)),
                  pl.BlockSpec((kk, tn), lambda i, j: (0, j)),
                  pl.BlockSpec((tm, tn), lambda i, j: (i, j))],
        out_specs=pl.BlockSpec((tm, tn), lambda i, j: (i, j)),
        out_shape=jax.ShapeDtypeStruct((t, d), F32),
        compiler_params=_params(("parallel", "arbitrary")),
        name="matmul_residual",
    )(a, w, x)


def _odd_mixer(x, b, s, positions, norm_g, w_qkv, lq1, lk1, lq2, lk2, subln_g, w_o, lambda_init):
    t = b * s
    qkv = _norm_matmul(x, norm_g, w_qkv.astype(BF16), tn=768)
    inv_freq = ROPE_THETA ** (-jnp.arange(0, ROPE_DIMS, 2, dtype=F32) / ROPE_DIMS)
    freq = jnp.concatenate([inv_freq, inv_freq, jnp.zeros((LANES - ROPE_DIMS,), F32)]).reshape(1, LANES)
    qkvb = _rope_cast(qkv, positions.reshape(t, 1), freq)
    attn = _diff_attention(qkvb, b, s, lq1, lk1, lq2, lk2, subln_g, lambda_init)
    return _matmul_residual(attn, w_o.astype(BF16), x)


def kernel(x, positions, norm_ffn1, ffn1_wg, ffn1_wu, ffn1_wd, norm_mix, norm_ffn2, ffn2_wg, ffn2_wu, ffn2_wd, ev_w_in, ev_conv_w, ev_conv_b, ev_ln_g, ev_ln_b, ev_short_w, ev_a_log, ev_dt_bias, ev_onorm_g, ev_w_out, od_w_qkv, od_lq1, od_lk1, od_lq2, od_lk2, od_subln_g, od_w_o, final_norm):
    b, s, d = x.shape
    depth = norm_ffn1.shape[0]
    xf = x.reshape(b * s, d)
    for i in range(depth):
        xf = _ffn(xf, norm_ffn1[i], ffn1_wg[i], ffn1_wu[i], ffn1_wd[i])
        j = i // 2
        if i % 2 == 0:
            xf = _even_mixer(xf, b, s, norm_mix[i], ev_w_in[j], ev_conv_w[j], ev_conv_b[j], ev_ln_g[j],
                             ev_ln_b[j], ev_short_w[j], ev_a_log[j], ev_dt_bias[j], ev_onorm_g[j],
                             ev_w_out[j])
        else:
            lambda_init = 0.8 - 0.6 * math.exp(-0.3 * i)
            xf = _odd_mixer(xf, b, s, positions, norm_mix[i], od_w_qkv[j], od_lq1[j], od_lk1[j],
                            od_lq2[j], od_lk2[j], od_subln_g[j], od_w_o[j], lambda_init)
        xf = _ffn(xf, norm_ffn2[i], ffn2_wg[i], ffn2_wu[i], ffn2_wd[i],
                  final_g=final_norm if i == depth - 1 else None)
    return xf.reshape(b, s, d)
```

```python
import functools
import math

import jax
import jax.numpy as jnp
from jax import lax
from jax.experimental import pallas as pl
from jax.experimental.pallas import tpu as pltpu

F32 = jnp.float32
BF16 = jnp.bfloat16
EPS = 1e-6

D_FF = 5632
CONV_CH = 1024
CONV_WIDTH = 31
DN_HEADS = 8
DN_DK = 128
DN_DV = 128
SHORT_CONV = 3
CHUNK = 64
DA_HEADS = 8
DA_DH = 128
ROPE_THETA = 500000.0
ROPE_DIMS = DA_DH // 4

LANES = 128
CONV_HALO = 16
SHORT_HALO = 8
EVEN_IN_PAD = 6272
SCAL_COL_BLOCK = 6144 // LANES
VMEM_LIMIT = 56 * 1024 * 1024
HIGHEST = lax.Precision.HIGHEST


def _sigmoid(x):
    return 1.0 / (1.0 + jnp.exp(-x))


def _silu(x):
    return x * _sigmoid(x)


def _softplus(x):
    return jnp.maximum(x, 0.0) + jnp.log(1.0 + jnp.exp(-jnp.abs(x)))


def _rms(x, g):
    ms = jnp.mean(x * x, axis=-1, keepdims=True)
    return x * lax.rsqrt(ms + EPS) * g


def _dot(a, b):
    return jnp.dot(a, b, preferred_element_type=F32)


def _dot_nt(a, b, precision=None):
    return lax.dot_general(a, b, (((1,), (1,)), ((), ())), precision=precision,
                           preferred_element_type=F32)


def _params(sem):
    return pltpu.CompilerParams(dimension_semantics=sem, vmem_limit_bytes=VMEM_LIMIT)


def _ffn_kernel(x_ref, g_ref, wg_ref, wu_ref, wd_ref, *rest, final):
    if final:
        fg_ref, o_ref, xn_ref, acc_ref = rest
    else:
        o_ref, xn_ref, acc_ref = rest
    j = pl.program_id(1)

    @pl.when(j == 0)
    def _():
        xn_ref[...] = _rms(x_ref[...], g_ref[...]).astype(BF16)
        acc_ref[...] = jnp.zeros_like(acc_ref)

    xn = xn_ref[...]
    a = _dot(xn, wg_ref[...])
    b = _dot(xn, wu_ref[...])
    h = (_silu(a) * b).astype(BF16)
    acc_ref[...] += _dot(h, wd_ref[...])

    @pl.when(j == pl.num_programs(1) - 1)
    def _():
        y = x_ref[...] + 0.5 * acc_ref[...]
        if final:
            y = _rms(y, fg_ref[...])
        o_ref[...] = y


def _ffn(x, g, wg, wu, wd, final_g=None):
    t, d = x.shape
    ff = wg.shape[1]
    tm = min(512, t)
    tf = 512
    final = final_g is not None
    in_specs = [
        pl.BlockSpec((tm, d), lambda i, j: (i, 0)),
        pl.BlockSpec((1, d), lambda i, j: (0, 0)),
        pl.BlockSpec((d, tf), lambda i, j: (0, j)),
        pl.BlockSpec((d, tf), lambda i, j: (0, j)),
        pl.BlockSpec((tf, d), lambda i, j: (j, 0)),
    ]
    args = [x, g.reshape(1, d), wg.astype(BF16), wu.astype(BF16), wd.astype(BF16)]
    if final:
        in_specs.append(pl.BlockSpec((1, d), lambda i, j: (0, 0)))
        args.append(final_g.reshape(1, d))
    return pl.pallas_call(
        functools.partial(_ffn_kernel, final=final),
        grid=(t // tm, ff // tf),
        in_specs=in_specs,
        out_specs=pl.BlockSpec((tm, d), lambda i, j: (i, 0)),
        out_shape=jax.ShapeDtypeStruct((t, d), F32),
        scratch_shapes=[pltpu.VMEM((tm, d), BF16), pltpu.VMEM((tm, d), F32)],
        compiler_params=_params(("parallel", "arbitrary")),
        name="ffn",
    )(*args)


def _norm_mm_kernel(x_ref, g_ref, w_ref, o_ref, xn_ref):
    @pl.when(pl.program_id(1) == 0)
    def _():
        xn_ref[...] = _rms(x_ref[...], g_ref[...]).astype(BF16)

    o_ref[...] = _dot(xn_ref[...], w_ref[...])


def _norm_matmul(x, g, w, tn):
    t, d = x.shape
    n = w.shape[1]
    tm = min(512, t)
    return pl.pallas_call(
        _norm_mm_kernel,
        grid=(t // tm, n // tn),
        in_specs=[
            pl.BlockSpec((tm, d), lambda i, j: (i, 0)),
            pl.BlockSpec((1, d), lambda i, j: (0, 0)),
            pl.BlockSpec((d, tn), lambda i, j: (0, j)),
        ],
        out_specs=pl.BlockSpec((tm, tn), lambda i, j: (i, j)),
        out_shape=jax.ShapeDtypeStruct((t, n), F32),
        scratch_shapes=[pltpu.VMEM((tm, d), BF16)],
        compiler_params=_params(("parallel", "arbitrary")),
        name="norm_matmul",
    )(x, g.reshape(1, d), w)


def _conv_kernel(pv, pg, hvp, hgp, hvn, hgn, w_ref, o_ref, scr, *, ts):
    i = pl.program_id(1)
    last = pl.num_programs(1) - 1
    prev = jnp.where(i > 0, hvp[0] * _sigmoid(hgp[0]), 0.0)
    nxt = jnp.where(i < last, hvn[0] * _sigmoid(hgn[0]), 0.0)
    scr[0:CONV_HALO, :] = prev
    scr[CONV_HALO:CONV_HALO + ts, :] = pv[0] * _sigmoid(pg[0])
    scr[CONV_HALO + ts:2 * CONV_HALO + ts, :] = nxt
    acc = w_ref[0:1, :] * scr[1:1 + ts, :]
    for k in range(1, CONV_WIDTH):
        acc = acc + w_ref[k:k + 1, :] * scr[1 + k:1 + k + ts, :]
    o_ref[0] = acc


def _glu_conv(p3, conv_w, ts):
    b, s, _ = p3.shape
    nc = CONV_CH // LANES
    r = ts // CONV_HALO
    nh = s // CONV_HALO

    def main(off):
        return pl.BlockSpec((1, ts, LANES), lambda bb, i, c: (bb, i, off + c))

    def prev(off):
        return pl.BlockSpec((1, CONV_HALO, LANES),
                            lambda bb, i, c: (bb, jnp.maximum(i * r - 1, 0), off + c))

    def nxt(off):
        return pl.BlockSpec((1, CONV_HALO, LANES),
                            lambda bb, i, c: (bb, jnp.minimum((i + 1) * r, nh - 1), off + c))

    return pl.pallas_call(
        functools.partial(_conv_kernel, ts=ts),
        grid=(b, s // ts, nc),
        in_specs=[main(0), main(nc), prev(0), prev(nc), nxt(0), nxt(nc),
                  pl.BlockSpec((CONV_WIDTH, LANES), lambda bb, i, c: (0, c))],
        out_specs=pl.BlockSpec((1, ts, LANES), lambda bb, i, c: (bb, i, c)),
        out_shape=jax.ShapeDtypeStruct((b, s, CONV_CH), F32),
        scratch_shapes=[pltpu.VMEM((ts + 2 * CONV_HALO, LANES), F32)],
        compiler_params=_params(("parallel", "parallel", "parallel")),
        name="glu_conv",
    )(p3, p3, p3, p3, p3, p3, conv_w)


def _short_conv_kernel(x_ref, hp, hn, w_ref, o_ref, scr, *, ts):
    i = pl.program_id(1)
    c = pl.program_id(2)
    last = pl.num_programs(1) - 1
    scr[0:SHORT_HALO, :] = jnp.where(i > 0, hp[0], 0.0)
    scr[SHORT_HALO:SHORT_HALO + ts, :] = x_ref[0]
    scr[SHORT_HALO + ts:2 * SHORT_HALO + ts, :] = jnp.where(i < last, hn[0], 0.0)
    y = (w_ref[0:1, :] * scr[SHORT_HALO - 1:SHORT_HALO - 1 + ts, :]
         + w_ref[1:2, :] * scr[SHORT_HALO:SHORT_HALO + ts, :]
         + w_ref[2:3, :] * scr[SHORT_HALO + 1:SHORT_HALO + 1 + ts, :])
    y = _silu(y)

    @pl.when(c == 2)
    def _():
        o_ref[0] = y

    @pl.when(c < 2)
    def _():
        scale = jnp.where(c == 0, DN_DK ** -0.5, 1.0)
        for h in range(DN_HEADS):
            sl = slice(h * DN_DK, (h + 1) * DN_DK)
            yh = y[:, sl]
            o_ref[0, :, sl] = yh * (lax.rsqrt(jnp.sum(yh * yh, axis=-1, keepdims=True) + EPS) * scale)


def _short_conv(p3, short_w, ts):
    b, s, _ = p3.shape
    w = DN_HEADS * DN_DK
    nc = short_w.shape[1] // w
    off = 2 * CONV_CH // w
    r = ts // SHORT_HALO
    nh = s // SHORT_HALO
    return pl.pallas_call(
        functools.partial(_short_conv_kernel, ts=ts),
        grid=(b, s // ts, nc),
        in_specs=[
            pl.BlockSpec((1, ts, w), lambda bb, i, c: (bb, i, off + c)),
            pl.BlockSpec((1, SHORT_HALO, w),
                         lambda bb, i, c: (bb, jnp.maximum(i * r - 1, 0), off + c)),
            pl.BlockSpec((1, SHORT_HALO, w),
                         lambda bb, i, c: (bb, jnp.minimum((i + 1) * r, nh - 1), off + c)),
            pl.BlockSpec((SHORT_CONV, w), lambda bb, i, c: (0, c)),
        ],
        out_specs=pl.BlockSpec((1, ts, w), lambda bb, i, c: (bb, i, c)),
        out_shape=jax.ShapeDtypeStruct((b, s, nc * w), F32),
        scratch_shapes=[pltpu.VMEM((ts + 2 * SHORT_HALO, w), F32)],
        compiler_params=_params(("parallel", "parallel", "parallel")),
        name="short_conv",
    )(p3, p3, p3, short_w)


def _delta_prep_kernel(q_ref, k_ref, v_ref, sc_ref, alog_ref, dtb_ref,
                       u_ref, wq_ref, akd_ref, gl_ref):
    sc = sc_ref[0]
    beta = _sigmoid(sc)
    araw = pltpu.roll(sc, LANES - 2 * DN_HEADS, 1)
    g = -jnp.exp(alog_ref[...]) * _softplus(araw + dtb_ref[...])

    ii = lax.broadcasted_iota(jnp.int32, (CHUNK, CHUNK), 0)
    jj = lax.broadcasted_iota(jnp.int32, (CHUNK, CHUNK), 1)
    low = (ii >= jj).astype(F32)
    upp = (ii <= jj).astype(F32)
    gcf = jnp.dot(low, g, precision=HIGHEST, preferred_element_type=F32)
    gcb = jnp.dot(upp, g, precision=HIGHEST, preferred_element_type=F32)
    lane = lax.broadcasted_iota(jnp.int32, (CHUNK, LANES), 1)
    gc = jnp.where(lane < DN_HEADS, gcf, gcb)
    gtot = jnp.broadcast_to(gcf[CHUNK - 1:CHUNK, :], (CHUNK, LANES))
    gexp = jnp.exp(gc)
    kdec = jnp.exp(gtot - gc)

    r128 = lax.broadcasted_iota(jnp.int32, (LANES, LANES), 0)
    c128 = lax.broadcasted_iota(jnp.int32, (LANES, LANES), 1)
    eye128 = (r128 == c128).astype(F32)
    gtt = _dot_nt(eye128, gtot, precision=HIGHEST)

    nch = 2 * DN_HEADS
    wcat = nch * CHUNK
    er = lax.broadcasted_iota(jnp.int32, (LANES, wcat), 0)
    el = lax.broadcasted_iota(jnp.int32, (LANES, wcat), 1)
    expand = (jnp.right_shift(el, 6) == er).astype(F32)
    gcat = jnp.dot(gc, expand, precision=HIGHEST, preferred_element_type=F32)
    bcat = jnp.dot(beta, expand, precision=HIGHEST, preferred_element_type=F32)
    tt = lax.broadcasted_iota(jnp.int32, (CHUNK, wcat), 0)
    ll = lax.broadcasted_iota(jnp.int32, (CHUNK, wcat), 1)
    jl = jnp.bitwise_and(ll, CHUNK - 1)
    diag = tt == jl
    ones = jnp.ones((CHUNK, CHUNK), F32)
    rowcat = jnp.dot(ones, jnp.where(diag, gcat, 0.0), precision=HIGHEST,
                     preferred_element_type=F32)
    ahead = jnp.where(ll < DN_HEADS * CHUNK, tt - jl, jl - tt)
    incl = ahead >= 0
    strict = ahead > 0
    dec = jnp.exp(jnp.where(incl, gcat - rowcat, -jnp.inf))

    kks, qks, ks, qs, vs = [], [], [], [], []
    for h in range(DN_HEADS):
        sl = slice(h * DN_DK, (h + 1) * DN_DK)
        qh = q_ref[0, :, sl]
        kh = k_ref[0, :, sl]
        khb = kh.astype(BF16)
        gram = _dot_nt(jnp.concatenate([khb, qh.astype(BF16)], axis=0), khb)
        kks.append(gram[0:CHUNK])
        qks.append(gram[CHUNK:])
        ks.append(kh)
        qs.append(qh)
        vs.append(v_ref[0, :, sl])
    kkcat = jnp.concatenate(kks + kks, axis=1)
    qkcat = jnp.concatenate(qks + qks, axis=1)
    mcat = jnp.where(strict, -(kkcat * bcat) * dec, 0.0)
    acat = jnp.where(incl, qkcat * dec, 0.0)

    gw = 4 * CHUNK
    br = lax.broadcasted_iota(jnp.int32, (gw, gw), 0)
    bc = lax.broadcasted_iota(jnp.int32, (gw, gw), 1)
    bmask = jnp.right_shift(br, 6) == jnp.right_shift(bc, 6)

    def blockdiag(xg):
        xb = xg.astype(BF16)
        return jnp.where(bmask, jnp.concatenate([xb] * 4, axis=0), jnp.zeros((), BF16))

    ngrp = wcat // gw
    xs = [mcat[:, g * gw:(g + 1) * gw] for g in range(ngrp)]
    tinv = [jnp.where(diag[:, g * gw:(g + 1) * gw], 1.0, 0.0) + xs[g] for g in range(ngrp)]
    xbd = [blockdiag(x) for x in xs]
    for _ in range(5):
        xs = [_dot(xs[g].astype(BF16), xbd[g]) for g in range(ngrp)]
        xbd = [blockdiag(x) for x in xs]
        tinv = [tinv[g] + _dot(tinv[g].astype(BF16), xbd[g]) for g in range(ngrp)]

    eye128b = eye128.astype(BF16)
    rhs, kds, qds = [], [], []
    for c in range(nch):
        h = c % DN_HEADS
        bcol = beta[:, c:c + 1]
        gex = gexp[:, c:c + 1]
        rhs.append(jnp.concatenate([vs[h] * bcol, ks[h] * (bcol * gex)], axis=1).astype(BF16))
        kds.append((ks[h] * kdec[:, c:c + 1]).astype(BF16))
        qds.append(qs[h] * gex)
    uws, kdts = [], []
    for c in range(nch):
        g, o = divmod(c, 4)
        tc = tinv[g][:, o * CHUNK:(o + 1) * CHUNK]
        uws.append(_dot(tc.astype(BF16), rhs[c]))
        kdts.append(_dot_nt(eye128b, kds[c]))
    for c in range(nch):
        u_ref[0, 0, c] = uws[c][:, 0:DN_DV]
        wq_ref[0, 0, c] = jnp.concatenate([uws[c][:, DN_DV:], qds[c]], axis=0).astype(BF16)
        akd_ref[0, 0, c] = jnp.concatenate([acat[:, c * CHUNK:(c + 1) * CHUNK], kdts[c]],
                                           axis=0).astype(BF16)
        glc = jnp.exp(gtt[c:c + 1, :])
        gl_ref[0, 0, c] = jnp.broadcast_to(jnp.concatenate([glc, glc], axis=1), (8, LANES))


def _delta_prep(qkvn, p3, a_log, dt_bias):
    b, s, _ = qkvn.shape
    n = s // CHUNK
    nch = 2 * DN_HEADS
    w = DN_HEADS * DN_DK

    def pad_row(a):
        return jnp.pad(a.reshape(1, nch), ((0, 0), (0, LANES - nch)))

    return pl.pallas_call(
        _delta_prep_kernel,
        grid=(b, n),
        in_specs=[
            pl.BlockSpec((1, CHUNK, w), lambda bb, i: (bb, i, 0)),
            pl.BlockSpec((1, CHUNK, w), lambda bb, i: (bb, i, 1)),
            pl.BlockSpec((1, CHUNK, w), lambda bb, i: (bb, i, 2)),
            pl.BlockSpec((1, CHUNK, LANES), lambda bb, i: (bb, i, SCAL_COL_BLOCK)),
            pl.BlockSpec((1, LANES), lambda bb, i: (0, 0)),
            pl.BlockSpec((1, LANES), lambda bb, i: (0, 0)),
        ],
        out_specs=[
            pl.BlockSpec((1, 1, nch, CHUNK, DN_DV), lambda bb, i: (bb, i, 0, 0, 0)),
            pl.BlockSpec((1, 1, nch, 2 * CHUNK, DN_DK), lambda bb, i: (bb, i, 0, 0, 0)),
            pl.BlockSpec((1, 1, nch, CHUNK + DN_DK, CHUNK), lambda bb, i: (bb, i, 0, 0, 0)),
            pl.BlockSpec((1, 1, nch, 8, LANES), lambda bb, i: (bb, i, 0, 0, 0)),
        ],
        out_shape=[
            jax.ShapeDtypeStruct((b, n, nch, CHUNK, DN_DV), F32),
            jax.ShapeDtypeStruct((b, n, nch, 2 * CHUNK, DN_DK), BF16),
            jax.ShapeDtypeStruct((b, n, nch, CHUNK + DN_DK, CHUNK), BF16),
            jax.ShapeDtypeStruct((b, n, nch, 8, LANES), F32),
        ],
        compiler_params=_params(("parallel", "parallel")),
        name="delta_prep",
    )(qkvn, qkvn, qkvn, p3, pad_row(a_log), pad_row(dt_bias))


def _delta_scan_kernel(uf, wqf, akdf, glf, ub, wqb, akdb, glb, of_ref, ob_ref, st_ref, *, hb):
    @pl.when(pl.program_id(2) == 0)
    def _():
        st_ref[...] = jnp.zeros_like(st_ref)

    for d, (u_r, wq_r, akd_r, gl_r, o_r) in enumerate(((uf, wqf, akdf, glf, of_ref),
                                                       (ub, wqb, akdb, glb, ob_ref))):
        for h in range(hb):
            ch = d * hb + h
            st = st_ref[ch]
            ws = _dot(wq_r[0, 0, h], st.astype(BF16))
            vnew = u_r[0, 0, h] - ws[0:CHUNK]
            r = _dot(akd_r[0, 0, h], vnew.astype(BF16))
            o_r[0, :, h * DN_DV:(h + 1) * DN_DV] = ws[CHUNK:] + r[0:CHUNK]
            st_ref[ch] = st * gl_r[0, 0, h, 0:1, :] + r[CHUNK:]


def _delta_scan(u, wq, akd, gl, hb=4):
    b, n = u.shape[0], u.shape[1]
    s = n * CHUNK
    ng = DN_HEADS // hb
    gb = ng

    def spec(shape, back):
        if back:
            return pl.BlockSpec(shape, lambda bb, g, i: (bb, n - 1 - i, gb + g) + (0,) * (len(shape) - 3))
        return pl.BlockSpec(shape, lambda bb, g, i: (bb, i, g) + (0,) * (len(shape) - 3))

    shapes = [(1, 1, hb, CHUNK, DN_DV), (1, 1, hb, 2 * CHUNK, DN_DK),
              (1, 1, hb, CHUNK + DN_DK, CHUNK), (1, 1, hb, 8, LANES)]
    in_specs = [spec(sh, False) for sh in shapes] + [spec(sh, True) for sh in shapes]
    ow = hb * DN_DV
    return pl.pallas_call(
        functools.partial(_delta_scan_kernel, hb=hb),
        grid=(b, ng, n),
        in_specs=in_specs,
        out_specs=[pl.BlockSpec((1, CHUNK, ow), lambda bb, g, i: (bb, i, g)),
                   pl.BlockSpec((1, CHUNK, ow), lambda bb, g, i: (bb, n - 1 - i, g))],
        out_shape=[jax.ShapeDtypeStruct((b, s, DN_HEADS * DN_DV), F32)] * 2,
        scratch_shapes=[pltpu.VMEM((2 * hb, DN_DK, DN_DV), F32)],
        compiler_params=_params(("parallel", "parallel", "arbitrary")),
        name="delta_scan",
    )(u, wq, akd, gl, u, wq, akd, gl)


def _even_out_kernel(cv_ref, cb_ref, lg_ref, lb_ref, of_ref, ob_ref, z_ref, og_ref, w_ref, x_ref,
                     o_ref, lhs_ref):
    @pl.when(pl.program_id(1) == 0)
    def _():
        a = cv_ref[...] + cb_ref[...]
        mu = jnp.mean(a, axis=-1, keepdims=True)
        xc = a - mu
        y = xc * lax.rsqrt(jnp.mean(xc * xc, axis=-1, keepdims=True) + EPS)
        lhs_ref[:, 0:CONV_CH] = _silu(y * lg_ref[...] + lb_ref[...]).astype(BF16)
        for h in range(DN_HEADS):
            sl = slice(h * DN_DV, (h + 1) * DN_DV)
            o = of_ref[:, sl] + ob_ref[:, sl]
            lhs_ref[:, CONV_CH + h * DN_DV:CONV_CH + (h + 1) * DN_DV] = (
                _rms(o, og_ref[...]) * _silu(z_ref[:, sl])).astype(BF16)

    o_ref[...] = x_ref[...] + _dot(lhs_ref[...], w_ref[...])


def _even_out(conv, conv_b, ln_g, ln_b, o_f, o_b, p, onorm_g, w_out, x):
    t, d = x.shape
    tm = min(512, t)
    tn = 1024
    kk = CONV_CH + DN_HEADS * DN_DV
    zblk = (2 * CONV_CH + 3 * DN_HEADS * DN_DK) // CONV_CH

    def row(n):
        return pl.BlockSpec((1, n), lambda i, j: (0, 0))

    def tile(n, jblk=0):
        return pl.BlockSpec((tm, n), lambda i, j: (i, jblk))

    return pl.pallas_call(
        _even_out_kernel,
        grid=(t // tm, d // tn),
        in_specs=[tile(CONV_CH), row(CONV_CH), row(CONV_CH), row(CONV_CH),
                  tile(CONV_CH), tile(CONV_CH), tile(CONV_CH, zblk), row(DN_DV),
                  pl.BlockSpec((kk, tn), lambda i, j: (0, j)),
                  pl.BlockSpec((tm, tn), lambda i, j: (i, j))],
        out_specs=pl.BlockSpec((tm, tn), lambda i, j: (i, j)),
        out_shape=jax.ShapeDtypeStruct((t, d), F32),
        scratch_shapes=[pltpu.VMEM((tm, kk), BF16)],
        compiler_params=_params(("parallel", "arbitrary")),
        name="even_out",
    )(conv, conv_b.reshape(1, -1), ln_g.reshape(1, -1), ln_b.reshape(1, -1), o_f, o_b, p,
      onorm_g.reshape(1, -1), w_out.astype(BF16), x)


def _even_mixer(x, b, s, norm_g, w_in, conv_w, conv_b, ln_g, ln_b, short_w, a_log, dt_bias,
                onorm_g, w_out):
    t = b * s
    w_pad = jnp.pad(w_in, ((0, 0), (0, EVEN_IN_PAD - w_in.shape[1]))).astype(BF16)
    p = _norm_matmul(x, norm_g, w_pad, tn=EVEN_IN_PAD // 7)
    p3 = p.reshape(b, s, EVEN_IN_PAD)
    ts = min(512, s)
    conv = _glu_conv(p3, conv_w, ts)
    qkvn = _short_conv(p3, short_w, ts)
    u, wq, akd, gl = _delta_prep(qkvn, p3, a_log, dt_bias)
    o_f, o_b = _delta_scan(u, wq, akd, gl)
    return _even_out(conv.reshape(t, -1), conv_b, ln_g, ln_b, o_f.reshape(t, -1), o_b.reshape(t, -1),
                     p, onorm_g, w_out, x)


def _rope_kernel(x_ref, pos_ref, freq_ref, o_ref, cs_ref):
    j = pl.program_id(1)
    nrep = x_ref.shape[1] // LANES

    @pl.when(j == 0)
    def _():
        ang = pos_ref[...].astype(F32) * freq_ref[...]
        lane = lax.broadcasted_iota(jnp.int32, ang.shape, 1)
        half = ROPE_DIMS // 2
        cs_ref[0] = jnp.where(lane < ROPE_DIMS, jnp.cos(ang), 1.0)
        sn = jnp.sin(ang)
        cs_ref[1] = jnp.where(lane < half, -sn, 0.0)
        cs_ref[2] = jnp.where((lane >= half) & (lane < ROPE_DIMS), sn, 0.0)

    x = x_ref[...]
    nq = DA_HEADS * 2 * DA_DH // x_ref.shape[1]

    @pl.when(j < 2 * nq)
    def _():
        half = ROPE_DIMS // 2
        w = x.shape[1]
        c = jnp.concatenate([cs_ref[0]] * nrep, axis=1)
        s1 = jnp.concatenate([cs_ref[1]] * nrep, axis=1)
        s2 = jnp.concatenate([cs_ref[2]] * nrep, axis=1)
        y = x * c + pltpu.roll(x, w - half, 1) * s1 + pltpu.roll(x, half, 1) * s2
        o_ref[...] = (y * jnp.where(j < nq, DA_DH ** -0.5 * math.log2(math.e), 1.0)).astype(BF16)

    @pl.when(j >= 2 * nq)
    def _():
        o_ref[...] = x.astype(BF16)


def _rope_cast(qkv, pos, freq):
    t, n = qkv.shape
    tm = min(512, t)
    tn = 1024
    return pl.pallas_call(
        _rope_kernel,
        grid=(t // tm, n // tn),
        in_specs=[pl.BlockSpec((tm, tn), lambda i, j: (i, j)),
                  pl.BlockSpec((tm, 1), lambda i, j: (i, 0)),
                  pl.BlockSpec((1, LANES), lambda i, j: (0, 0))],
        out_specs=pl.BlockSpec((tm, tn), lambda i, j: (i, j)),
        out_shape=jax.ShapeDtypeStruct((t, n), BF16),
        scratch_shapes=[pltpu.VMEM((3, tm, LANES), F32)],
        compiler_params=_params(("parallel", "arbitrary")),
        name="rope_cast",
    )(qkv, pos, freq)


def _attn_kernel(q_ref, k_ref, v_ref, lq1, lk1, lq2, lk2, sg_ref, o_ref, *, lambda_init):
    lam = (jnp.exp(jnp.sum(lq1[...] * lk1[...], axis=-1, keepdims=True))
           - jnp.exp(jnp.sum(lq2[...] * lk2[...], axis=-1, keepdims=True)) + lambda_init)
    q = q_ref[...]
    k = k_ref[...]
    v = v_ref[...]
    s1 = _dot_nt(q[:, 0:DA_DH], k[:, 0:DA_DH])
    s2 = _dot_nt(q[:, DA_DH:], k[:, DA_DH:])
    e1 = jnp.exp2(s1 - jnp.max(s1, axis=-1, keepdims=True))
    e2 = jnp.exp2(s2 - jnp.max(s2, axis=-1, keepdims=True))
    r1 = 1.0 / jnp.sum(e1, axis=-1, keepdims=True)
    r2 = lam / jnp.sum(e2, axis=-1, keepdims=True)
    o = _dot(e1.astype(BF16), v) * r1 - _dot(e2.astype(BF16), v) * r2
    o_ref[...] = (_rms(o, sg_ref[...]) * (1.0 - lambda_init)).astype(BF16)


def _diff_attention(qkvb, b, s, lq1, lk1, lq2, lk2, subln_g, lambda_init):
    t = b * s
    hw = 2 * DA_DH
    tq = min(256, s)
    nq = s // tq

    def row(n):
        return pl.BlockSpec((1, n), lambda bb, h, i: (0, 0))

    return pl.pallas_call(
        functools.partial(_attn_kernel, lambda_init=lambda_init),
        grid=(b, DA_HEADS, nq),
        in_specs=[pl.BlockSpec((tq, hw), lambda bb, h, i: (bb * nq + i, h)),
                  pl.BlockSpec((s, hw), lambda bb, h, i: (bb, DA_HEADS + h)),
                  pl.BlockSpec((s, hw), lambda bb, h, i: (bb, 2 * DA_HEADS + h)),
                  row(DA_DH), row(DA_DH), row(DA_DH), row(DA_DH), row(hw)],
        out_specs=pl.BlockSpec((tq, hw), lambda bb, h, i: (bb * nq + i, h)),
        out_shape=jax.ShapeDtypeStruct((t, DA_HEADS * hw), BF16),
        compiler_params=_params(("parallel", "parallel", "arbitrary")),
        name="diff_attention",
    )(qkvb, qkvb, qkvb, lq1.reshape(1, -1), lk1.reshape(1, -1), lq2.reshape(1, -1),
      lk2.reshape(1, -1), subln_g.reshape(1, -1))


def _mm_res_kernel(a_ref, w_ref, x_ref, o_ref):
    o_ref[...] = x_ref[...] + _dot(a_ref[...], w_ref[...])


def _matmul_residual(a, w, x):
    t, d = x.shape
    kk = a.shape[1]
    tm = min(512, t)
    tn = 1024
    return pl.pallas_call(
        _mm_res_kernel,
        grid=(t // tm, d // tn),
        in_specs=[pl.BlockSpec((tm, kk), lambda i, j: (i, 0)),
                  pl.BlockSpec((kk, tn), lambda i, j: (0, j)),
                  pl.BlockSpec((tm, tn), lambda i, j: (i, j))],
        out_specs=pl.BlockSpec((tm, tn), lambda i, j: (i, j)),
        out_shape=jax.ShapeDtypeStruct((t, d), F32),
        compiler_params=_params(("parallel", "arbitrary")),
        name="matmul_residual",
    )(a, w, x)


def _odd_mixer(x, b, s, positions, norm_g, w_qkv, lq1, lk1, lq2, lk2, subln_g, w_o, lambda_init):
    t = b * s
    qkv = _norm_matmul(x, norm_g, w_qkv.astype(BF16), tn=768)
    inv_freq = ROPE_THETA ** (-jnp.arange(0, ROPE_DIMS, 2, dtype=F32) / ROPE_DIMS)
    freq = jnp.concatenate([inv_freq, inv_freq, jnp.zeros((LANES - ROPE_DIMS,), F32)]).reshape(1, LANES)
    qkvb = _rope_cast(qkv, positions.reshape(t, 1), freq)
    attn = _diff_attention(qkvb, b, s, lq1, lk1, lq2, lk2, subln_g, lambda_init)
    return _matmul_residual(attn, w_o.astype(BF16), x)


def kernel(x, positions, norm_ffn1, ffn1_wg, ffn1_wu, ffn1_wd, norm_mix, norm_ffn2, ffn2_wg, ffn2_wu, ffn2_wd, ev_w_in, ev_conv_w, ev_conv_b, ev_ln_g, ev_ln_b, ev_short_w, ev_a_log, ev_dt_bias, ev_onorm_g, ev_w_out, od_w_qkv, od_lq1, od_lk1, od_lq2, od_lk2, od_subln_g, od_w_o, final_norm):
    b, s, d = x.shape
    depth = norm_ffn1.shape[0]
    xf = x.reshape(b * s, d)
    for i in range(depth):
        xf = _ffn(xf, norm_ffn1[i], ffn1_wg[i], ffn1_wu[i], ffn1_wd[i])
        j = i // 2
        if i % 2 == 0:
            xf = _even_mixer(xf, b, s, norm_mix[i], ev_w_in[j], ev_conv_w[j], ev_conv_b[j], ev_ln_g[j],
                             ev_ln_b[j], ev_short_w[j], ev_a_log[j], ev_dt_bias[j], ev_onorm_g[j],
                             ev_w_out[j])
        else:
            lambda_init = 0.8 - 0.6 * math.exp(-0.3 * i)
            xf = _odd_mixer(xf, b, s, positions, norm_mix[i], od_w_qkv[j], od_lq1[j], od_lk1[j],
                            od_lq2[j], od_lk2[j], od_subln_g[j], od_w_o[j], lambda_init)
        xf = _ffn(xf, norm_ffn2[i], ffn2_wg[i], ffn2_wu[i], ffn2_wd[i],
                  final_g=final_norm if i == depth - 1 else None)
    return xf.reshape(b, s, d)
```

```python
import functools
import math

import jax
import jax.numpy as jnp
from jax import lax
from jax.experimental import pallas as pl
from jax.experimental.pallas import tpu as pltpu

F32 = jnp.float32
BF16 = jnp.bfloat16
EPS = 1e-6

D_FF = 5632
CONV_CH = 1024
CONV_WIDTH = 31
DN_HEADS = 8
DN_DK = 128
DN_DV = 128
SHORT_CONV = 3
CHUNK = 64
DA_HEADS = 8
DA_DH = 128
ROPE_THETA = 500000.0
ROPE_DIMS = DA_DH // 4

LANES = 128
CONV_HALO = 16
SHORT_HALO = 8
MXU_WIDTH = 256
EVEN_IN_PAD = 6400
SCAL_COL_BLOCK = 6144 // LANES
VMEM_LIMIT = 56 * 1024 * 1024
HIGHEST = lax.Precision.HIGHEST


def _sigmoid(x):
    return 1.0 / (1.0 + jnp.exp(-x))


def _silu(x):
    return x * _sigmoid(x)


def _softplus(x):
    return jnp.maximum(x, 0.0) + jnp.log(1.0 + jnp.exp(-jnp.abs(x)))


def _rms(x, g):
    ms = jnp.mean(x * x, axis=-1, keepdims=True)
    return x * lax.rsqrt(ms + EPS) * g


def _dot(a, b):
    return jnp.dot(a, b, preferred_element_type=F32)


def _dot_nt(a, b, precision=None):
    return lax.dot_general(a, b, (((1,), (1,)), ((), ())), precision=precision,
                           preferred_element_type=F32)


def _params(sem):
    return pltpu.CompilerParams(dimension_semantics=sem, vmem_limit_bytes=VMEM_LIMIT)


def _ffn_kernel(x_ref, g_ref, wg_ref, wu_ref, wd_ref, *rest, final):
    if final:
        fg_ref, o_ref, xn_ref, acc_ref = rest
    else:
        o_ref, xn_ref, acc_ref = rest
    j = pl.program_id(1)

    @pl.when(j == 0)
    def _():
        xn_ref[...] = _rms(x_ref[...], g_ref[...]).astype(BF16)
        acc_ref[...] = jnp.zeros_like(acc_ref)

    xn = xn_ref[...]
    a = _dot(xn, wg_ref[...])
    b = _dot(xn, wu_ref[...])
    h = (_silu(a) * b).astype(BF16)
    acc_ref[...] += _dot(h, wd_ref[...])

    @pl.when(j == pl.num_programs(1) - 1)
    def _():
        y = x_ref[...] + 0.5 * acc_ref[...]
        if final:
            y = _rms(y, fg_ref[...])
        o_ref[...] = y


def _ffn(x, g, wg, wu, wd, final_g=None):
    t, d = x.shape
    ff = wg.shape[1]
    tm = min(512, t)
    tf = 512
    final = final_g is not None
    in_specs = [
        pl.BlockSpec((tm, d), lambda i, j: (i, 0)),
        pl.BlockSpec((1, d), lambda i, j: (0, 0)),
        pl.BlockSpec((d, tf), lambda i, j: (0, j)),
        pl.BlockSpec((d, tf), lambda i, j: (0, j)),
        pl.BlockSpec((tf, d), lambda i, j: (j, 0)),
    ]
    args = [x, g.reshape(1, d), wg.astype(BF16), wu.astype(BF16), wd.astype(BF16)]
    if final:
        in_specs.append(pl.BlockSpec((1, d), lambda i, j: (0, 0)))
        args.append(final_g.reshape(1, d))
    return pl.pallas_call(
        functools.partial(_ffn_kernel, final=final),
        grid=(t // tm, ff // tf),
        in_specs=in_specs,
        out_specs=pl.BlockSpec((tm, d), lambda i, j: (i, 0)),
        out_shape=jax.ShapeDtypeStruct((t, d), F32),
        scratch_shapes=[pltpu.VMEM((tm, d), BF16), pltpu.VMEM((tm, d), F32)],
        compiler_params=_params(("parallel", "arbitrary")),
        name="ffn",
    )(*args)


def _norm_mm_kernel(x_ref, g_ref, w_ref, o_ref, xn_ref):
    @pl.when(pl.program_id(1) == 0)
    def _():
        xn_ref[...] = _rms(x_ref[...], g_ref[...]).astype(BF16)

    o_ref[...] = _dot(xn_ref[...], w_ref[...])


def _norm_matmul(x, g, w, tn):
    t, d = x.shape
    n = w.shape[1]
    tm = min(1024, t)
    return pl.pallas_call(
        _norm_mm_kernel,
        grid=(t // tm, n // tn),
        in_specs=[
            pl.BlockSpec((tm, d), lambda i, j: (i, 0)),
            pl.BlockSpec((1, d), lambda i, j: (0, 0)),
            pl.BlockSpec((d, tn), lambda i, j: (0, j)),
        ],
        out_specs=pl.BlockSpec((tm, tn), lambda i, j: (i, j)),
        out_shape=jax.ShapeDtypeStruct((t, n), F32),
        scratch_shapes=[pltpu.VMEM((tm, d), BF16)],
        compiler_params=_params(("parallel", "arbitrary")),
        name="norm_matmul",
    )(x, g.reshape(1, d), w)


def _conv_kernel(pv, pg, hvp, hgp, hvn, hgn, w_ref, o_ref, scr, *, ts):
    i = pl.program_id(1)
    last = pl.num_programs(1) - 1
    prev = jnp.where(i > 0, hvp[0] * _sigmoid(hgp[0]), 0.0)
    nxt = jnp.where(i < last, hvn[0] * _sigmoid(hgn[0]), 0.0)
    scr[0:CONV_HALO, :] = prev
    scr[CONV_HALO:CONV_HALO + ts, :] = pv[0] * _sigmoid(pg[0])
    scr[CONV_HALO + ts:2 * CONV_HALO + ts, :] = nxt
    acc = w_ref[0:1, :] * scr[1:1 + ts, :]
    for k in range(1, CONV_WIDTH):
        acc = acc + w_ref[k:k + 1, :] * scr[1 + k:1 + k + ts, :]
    o_ref[0] = acc


def _glu_conv(p3, conv_w, ts):
    b, s, _ = p3.shape
    nc = CONV_CH // LANES
    r = ts // CONV_HALO
    nh = s // CONV_HALO

    def main(off):
        return pl.BlockSpec((1, ts, LANES), lambda bb, i, c: (bb, i, off + c))

    def prev(off):
        return pl.BlockSpec((1, CONV_HALO, LANES),
                            lambda bb, i, c: (bb, jnp.maximum(i * r - 1, 0), off + c))

    def nxt(off):
        return pl.BlockSpec((1, CONV_HALO, LANES),
                            lambda bb, i, c: (bb, jnp.minimum((i + 1) * r, nh - 1), off + c))

    return pl.pallas_call(
        functools.partial(_conv_kernel, ts=ts),
        grid=(b, s // ts, nc),
        in_specs=[main(0), main(nc), prev(0), prev(nc), nxt(0), nxt(nc),
                  pl.BlockSpec((CONV_WIDTH, LANES), lambda bb, i, c: (0, c))],
        out_specs=pl.BlockSpec((1, ts, LANES), lambda bb, i, c: (bb, i, c)),
        out_shape=jax.ShapeDtypeStruct((b, s, CONV_CH), F32),
        scratch_shapes=[pltpu.VMEM((ts + 2 * CONV_HALO, LANES), F32)],
        compiler_params=_params(("parallel", "parallel", "parallel")),
        name="glu_conv",
    )(p3, p3, p3, p3, p3, p3, conv_w)


def _short_conv_kernel(x_ref, hp, hn, w_ref, o_ref, scr, *, ts):
    i = pl.program_id(1)
    c = pl.program_id(2)
    last = pl.num_programs(1) - 1
    scr[0:SHORT_HALO, :] = jnp.where(i > 0, hp[0], 0.0)
    scr[SHORT_HALO:SHORT_HALO + ts, :] = x_ref[0]
    scr[SHORT_HALO + ts:2 * SHORT_HALO + ts, :] = jnp.where(i < last, hn[0], 0.0)
    y = (w_ref[0:1, :] * scr[SHORT_HALO - 1:SHORT_HALO - 1 + ts, :]
         + w_ref[1:2, :] * scr[SHORT_HALO:SHORT_HALO + ts, :]
         + w_ref[2:3, :] * scr[SHORT_HALO + 1:SHORT_HALO + 1 + ts, :])
    y = _silu(y)

    @pl.when(c == 2)
    def _():
        o_ref[0] = y

    @pl.when(c < 2)
    def _():
        scale = jnp.where(c == 0, DN_DK ** -0.5, 1.0)
        for h in range(DN_HEADS):
            sl = slice(h * DN_DK, (h + 1) * DN_DK)
            yh = y[:, sl]
            o_ref[0, :, sl] = yh * (lax.rsqrt(jnp.sum(yh * yh, axis=-1, keepdims=True) + EPS) * scale)


def _short_conv(p3, short_w, ts):
    b, s, _ = p3.shape
    w = DN_HEADS * DN_DK
    nc = short_w.shape[1] // w
    off = 2 * CONV_CH // w
    r = ts // SHORT_HALO
    nh = s // SHORT_HALO
    return pl.pallas_call(
        functools.partial(_short_conv_kernel, ts=ts),
        grid=(b, s // ts, nc),
        in_specs=[
            pl.BlockSpec((1, ts, w), lambda bb, i, c: (bb, i, off + c)),
            pl.BlockSpec((1, SHORT_HALO, w),
                         lambda bb, i, c: (bb, jnp.maximum(i * r - 1, 0), off + c)),
            pl.BlockSpec((1, SHORT_HALO, w),
                         lambda bb, i, c: (bb, jnp.minimum((i + 1) * r, nh - 1), off + c)),
            pl.BlockSpec((SHORT_CONV, w), lambda bb, i, c: (0, c)),
        ],
        out_specs=pl.BlockSpec((1, ts, w), lambda bb, i, c: (bb, i, c)),
        out_shape=jax.ShapeDtypeStruct((b, s, nc * w), F32),
        scratch_shapes=[pltpu.VMEM((ts + 2 * SHORT_HALO, w), F32)],
        compiler_params=_params(("parallel", "parallel", "parallel")),
        name="short_conv",
    )(p3, p3, p3, short_w)


def _delta_prep_kernel(q_ref, k_ref, v_ref, sc_ref, alog_ref, dtb_ref,
                       u_ref, wq_ref, akd_ref, gl_ref):
    sc = sc_ref[0]
    beta = _sigmoid(sc)
    araw = pltpu.roll(sc, LANES - 2 * DN_HEADS, 1)
    g = -jnp.exp(alog_ref[...]) * _softplus(araw + dtb_ref[...])

    ii = lax.broadcasted_iota(jnp.int32, (CHUNK, CHUNK), 0)
    jj = lax.broadcasted_iota(jnp.int32, (CHUNK, CHUNK), 1)
    low = (ii >= jj).astype(F32)
    upp = (ii <= jj).astype(F32)
    gcf = jnp.dot(low, g, precision=HIGHEST, preferred_element_type=F32)
    gcb = jnp.dot(upp, g, precision=HIGHEST, preferred_element_type=F32)
    lane = lax.broadcasted_iota(jnp.int32, (CHUNK, LANES), 1)
    gc = jnp.where(lane < DN_HEADS, gcf, gcb)
    gtot = jnp.broadcast_to(gcf[CHUNK - 1:CHUNK, :], (CHUNK, LANES))
    gexp = jnp.exp(gc)
    kdec = jnp.exp(gtot - gc)

    r128 = lax.broadcasted_iota(jnp.int32, (LANES, LANES), 0)
    c128 = lax.broadcasted_iota(jnp.int32, (LANES, LANES), 1)
    eye128 = (r128 == c128).astype(F32)
    gtt = _dot_nt(eye128, gtot, precision=HIGHEST)

    nch = 2 * DN_HEADS
    wcat = nch * CHUNK
    er = lax.broadcasted_iota(jnp.int32, (LANES, wcat), 0)
    el = lax.broadcasted_iota(jnp.int32, (LANES, wcat), 1)
    expand = (jnp.right_shift(el, 6) == er).astype(F32)
    gcat = jnp.dot(gc, expand, precision=HIGHEST, preferred_element_type=F32)
    bcat = jnp.dot(beta, expand, precision=HIGHEST, preferred_element_type=F32)
    tt = lax.broadcasted_iota(jnp.int32, (CHUNK, wcat), 0)
    ll = lax.broadcasted_iota(jnp.int32, (CHUNK, wcat), 1)
    jl = jnp.bitwise_and(ll, CHUNK - 1)
    diag = tt == jl
    ones = jnp.ones((CHUNK, CHUNK), F32)
    rowcat = jnp.dot(ones, jnp.where(diag, gcat, 0.0), precision=HIGHEST,
                     preferred_element_type=F32)
    ahead = jnp.where(ll < DN_HEADS * CHUNK, tt - jl, jl - tt)
    incl = ahead >= 0
    strict = ahead > 0
    dec = jnp.exp(jnp.where(incl, gcat - rowcat, -jnp.inf))

    kks, qks, ks, qs, vs = [], [], [], [], []
    for h in range(DN_HEADS):
        sl = slice(h * DN_DK, (h + 1) * DN_DK)
        qh = q_ref[0, :, sl]
        kh = k_ref[0, :, sl]
        khb = kh.astype(BF16)
        gram = _dot_nt(jnp.concatenate([khb, qh.astype(BF16)], axis=0), khb)
        kks.append(gram[0:CHUNK])
        qks.append(gram[CHUNK:])
        ks.append(kh)
        qs.append(qh)
        vs.append(v_ref[0, :, sl])
    kkcat = jnp.concatenate(kks + kks, axis=1)
    qkcat = jnp.concatenate(qks + qks, axis=1)
    mcat = jnp.where(strict, -(kkcat * bcat) * dec, 0.0)
    acat = jnp.where(incl, qkcat * dec, 0.0)

    gw = 4 * CHUNK
    br = lax.broadcasted_iota(jnp.int32, (gw, gw), 0)
    bc = lax.broadcasted_iota(jnp.int32, (gw, gw), 1)
    bmask = jnp.right_shift(br, 6) == jnp.right_shift(bc, 6)

    def blockdiag(xg):
        xb = xg.astype(BF16)
        return jnp.where(bmask, jnp.concatenate([xb] * 4, axis=0), jnp.zeros((), BF16))

    ngrp = wcat // gw
    xs = [mcat[:, g * gw:(g + 1) * gw] for g in range(ngrp)]
    tinv = [jnp.where(diag[:, g * gw:(g + 1) * gw], 1.0, 0.0) + xs[g] for g in range(ngrp)]
    xbd = [blockdiag(x) for x in xs]
    for _ in range(5):
        xs = [_dot(xs[g].astype(BF16), xbd[g]) for g in range(ngrp)]
        xbd = [blockdiag(x) for x in xs]
        tinv = [tinv[g] + _dot(tinv[g].astype(BF16), xbd[g]) for g in range(ngrp)]

    eye128b = eye128.astype(BF16)
    rhs, kds, qds = [], [], []
    for c in range(nch):
        h = c % DN_HEADS
        bcol = beta[:, c:c + 1]
        gex = gexp[:, c:c + 1]
        rhs.append(jnp.concatenate([vs[h] * bcol, ks[h] * (bcol * gex)], axis=1).astype(BF16))
        kds.append((ks[h] * kdec[:, c:c + 1]).astype(BF16))
        qds.append(qs[h] * gex)
    uws, kdts = [], []
    for c in range(nch):
        g, o = divmod(c, 4)
        tc = tinv[g][:, o * CHUNK:(o + 1) * CHUNK]
        uws.append(_dot(tc.astype(BF16), rhs[c]))
        kdts.append(_dot_nt(eye128b, kds[c]))
    for c in range(nch):
        u_ref[0, 0, c] = uws[c][:, 0:DN_DV]
        wq_ref[0, 0, c] = jnp.concatenate([uws[c][:, DN_DV:], qds[c]], axis=0).astype(BF16)
        akd_ref[0, 0, c] = jnp.concatenate([acat[:, c * CHUNK:(c + 1) * CHUNK], kdts[c]],
                                           axis=0).astype(BF16)
        glc = jnp.exp(gtt[c:c + 1, :])
        gl_ref[0, 0, c] = jnp.broadcast_to(jnp.concatenate([glc, glc], axis=1), (8, LANES))


def _delta_prep(qkvn, p3, a_log, dt_bias):
    b, s, _ = qkvn.shape
    n = s // CHUNK
    nch = 2 * DN_HEADS
    w = DN_HEADS * DN_DK

    def pad_row(a):
        return jnp.pad(a.reshape(1, nch), ((0, 0), (0, LANES - nch)))

    return pl.pallas_call(
        _delta_prep_kernel,
        grid=(b, n),
        in_specs=[
            pl.BlockSpec((1, CHUNK, w), lambda bb, i: (bb, i, 0)),
            pl.BlockSpec((1, CHUNK, w), lambda bb, i: (bb, i, 1)),
            pl.BlockSpec((1, CHUNK, w), lambda bb, i: (bb, i, 2)),
            pl.BlockSpec((1, CHUNK, LANES), lambda bb, i: (bb, i, SCAL_COL_BLOCK)),
            pl.BlockSpec((1, LANES), lambda bb, i: (0, 0)),
            pl.BlockSpec((1, LANES), lambda bb, i: (0, 0)),
        ],
        out_specs=[
            pl.BlockSpec((1, 1, nch, CHUNK, DN_DV), lambda bb, i: (bb, i, 0, 0, 0)),
            pl.BlockSpec((1, 1, nch, 2 * CHUNK, DN_DK), lambda bb, i: (bb, i, 0, 0, 0)),
            pl.BlockSpec((1, 1, nch, CHUNK + DN_DK, CHUNK), lambda bb, i: (bb, i, 0, 0, 0)),
            pl.BlockSpec((1, 1, nch, 8, LANES), lambda bb, i: (bb, i, 0, 0, 0)),
        ],
        out_shape=[
            jax.ShapeDtypeStruct((b, n, nch, CHUNK, DN_DV), F32),
            jax.ShapeDtypeStruct((b, n, nch, 2 * CHUNK, DN_DK), BF16),
            jax.ShapeDtypeStruct((b, n, nch, CHUNK + DN_DK, CHUNK), BF16),
            jax.ShapeDtypeStruct((b, n, nch, 8, LANES), F32),
        ],
        compiler_params=_params(("parallel", "parallel")),
        name="delta_prep",
    )(qkvn, qkvn, qkvn, p3, pad_row(a_log), pad_row(dt_bias))


def _delta_scan_kernel(uf, wqf, akdf, glf, ub, wqb, akdb, glb, of_ref, ob_ref, st_ref, *, cb):
    @pl.when(pl.program_id(1) == 0)
    def _():
        st_ref[...] = jnp.zeros_like(st_ref)

    for step in range(cb):
        for d, (u_r, wq_r, akd_r, gl_r, o_r) in enumerate(((uf, wqf, akdf, glf, of_ref),
                                                           (ub, wqb, akdb, glb, ob_ref))):
            ci = step if d == 0 else cb - 1 - step
            for h in range(DN_HEADS):
                ch = d * DN_HEADS + h
                st = st_ref[ch]
                ws = _dot(wq_r[0, ci, h], st.astype(BF16))
                vnew = u_r[0, ci, h] - ws[0:CHUNK]
                r = _dot(akd_r[0, ci, h], vnew.astype(BF16))
                o_r[0, ci * CHUNK:(ci + 1) * CHUNK, h * DN_DV:(h + 1) * DN_DV] = ws[CHUNK:] + r[0:CHUNK]
                st_ref[ch] = st * gl_r[0, ci, h, 0:1, :] + r[CHUNK:]


def _delta_scan(u, wq, akd, gl, cb=2):
    b, n = u.shape[0], u.shape[1]
    s = n * CHUNK
    nb = n // cb
    hb = DN_HEADS

    def spec(shape, back):
        if back:
            return pl.BlockSpec(shape, lambda bb, i: (bb, nb - 1 - i, 1) + (0,) * (len(shape) - 3))
        return pl.BlockSpec(shape, lambda bb, i: (bb, i, 0) + (0,) * (len(shape) - 3))

    shapes = [(1, cb, hb, CHUNK, DN_DV), (1, cb, hb, 2 * CHUNK, DN_DK),
              (1, cb, hb, CHUNK + DN_DK, CHUNK), (1, cb, hb, 8, LANES)]
    in_specs = [spec(sh, False) for sh in shapes] + [spec(sh, True) for sh in shapes]
    ow = hb * DN_DV
    return pl.pallas_call(
        functools.partial(_delta_scan_kernel, cb=cb),
        grid=(b, nb),
        in_specs=in_specs,
        out_specs=[pl.BlockSpec((1, cb * CHUNK, ow), lambda bb, i: (bb, i, 0)),
                   pl.BlockSpec((1, cb * CHUNK, ow), lambda bb, i: (bb, nb - 1 - i, 0))],
        out_shape=[jax.ShapeDtypeStruct((b, s, ow), F32)] * 2,
        scratch_shapes=[pltpu.VMEM((2 * hb, DN_DK, DN_DV), F32)],
        compiler_params=_params(("parallel", "arbitrary")),
        name="delta_scan",
    )(u, wq, akd, gl, u, wq, akd, gl)


def _even_out_kernel(cv_ref, cb_ref, lg_ref, lb_ref, of_ref, ob_ref, z_ref, og_ref, w_ref, x_ref,
                     o_ref, lhs_ref):
    @pl.when(pl.program_id(1) == 0)
    def _():
        a = cv_ref[...] + cb_ref[...]
        mu = jnp.mean(a, axis=-1, keepdims=True)
        xc = a - mu
        y = xc * lax.rsqrt(jnp.mean(xc * xc, axis=-1, keepdims=True) + EPS)
        lhs_ref[:, 0:CONV_CH] = _silu(y * lg_ref[...] + lb_ref[...]).astype(BF16)
        for h in range(DN_HEADS):
            sl = slice(h * DN_DV, (h + 1) * DN_DV)
            o = of_ref[:, sl] + ob_ref[:, sl]
            lhs_ref[:, CONV_CH + h * DN_DV:CONV_CH + (h + 1) * DN_DV] = (
                _rms(o, og_ref[...]) * _silu(z_ref[:, sl])).astype(BF16)

    o_ref[...] = x_ref[...] + _dot(lhs_ref[...], w_ref[...])


def _even_out(conv, conv_b, ln_g, ln_b, o_f, o_b, p, onorm_g, w_out, x):
    t, d = x.shape
    tm = min(512, t)
    tn = 1024
    kk = CONV_CH + DN_HEADS * DN_DV
    zblk = (2 * CONV_CH + 3 * DN_HEADS * DN_DK) // CONV_CH

    def row(n):
        return pl.BlockSpec((1, n), lambda i, j: (0, 0))

    def tile(n, jblk=0):
        return pl.BlockSpec((tm, n), lambda i, j: (i, jblk))

    return pl.pallas_call(
        _even_out_kernel,
        grid=(t // tm, d // tn),
        in_specs=[tile(CONV_CH), row(CONV_CH), row(CONV_CH), row(CONV_CH),
                  tile(CONV_CH), tile(CONV_CH), tile(CONV_CH, zblk), row(DN_DV),
                  pl.BlockSpec((kk, tn), lambda i, j: (0, j)),
                  pl.BlockSpec((tm, tn), lambda i, j: (i, j))],
        out_specs=pl.BlockSpec((tm, tn), lambda i, j: (i, j)),
        out_shape=jax.ShapeDtypeStruct((t, d), F32),
        scratch_shapes=[pltpu.VMEM((tm, kk), BF16)],
        compiler_params=_params(("parallel", "arbitrary")),
        name="even_out",
    )(conv, conv_b.reshape(1, -1), ln_g.reshape(1, -1), ln_b.reshape(1, -1), o_f, o_b, p,
      onorm_g.reshape(1, -1), w_out.astype(BF16), x)


def _even_mixer(x, b, s, norm_g, w_in, conv_w, conv_b, ln_g, ln_b, short_w, a_log, dt_bias,
                onorm_g, w_out):
    t = b * s
    w_pad = jnp.pad(w_in, ((0, 0), (0, EVEN_IN_PAD - w_in.shape[1]))).astype(BF16)
    p = _norm_matmul(x, norm_g, w_pad, tn=5 * MXU_WIDTH)
    p3 = p.reshape(b, s, EVEN_IN_PAD)
    ts = min(512, s)
    conv = _glu_conv(p3, conv_w, ts)
    qkvn = _short_conv(p3, short_w, ts)
    u, wq, akd, gl = _delta_prep(qkvn, p3, a_log, dt_bias)
    o_f, o_b = _delta_scan(u, wq, akd, gl)
    return _even_out(conv.reshape(t, -1), conv_b, ln_g, ln_b, o_f.reshape(t, -1), o_b.reshape(t, -1),
                     p, onorm_g, w_out, x)


def _qkv_rope_kernel(xin_ref, g_ref, w_ref, pos_ref, freq_ref, o_ref, xn_ref, cs_ref):
    j = pl.program_id(1)
    nrep = o_ref.shape[1] // LANES

    @pl.when(j == 0)
    def _():
        xn_ref[...] = _rms(xin_ref[...], g_ref[...]).astype(BF16)
        ang = pos_ref[...].astype(F32) * freq_ref[...]
        lane = lax.broadcasted_iota(jnp.int32, ang.shape, 1)
        half = ROPE_DIMS // 2
        cs_ref[0] = jnp.where(lane < ROPE_DIMS, jnp.cos(ang), 1.0)
        sn = jnp.sin(ang)
        cs_ref[1] = jnp.where(lane < half, -sn, 0.0)
        cs_ref[2] = jnp.where((lane >= half) & (lane < ROPE_DIMS), sn, 0.0)

    x = _dot(xn_ref[...], w_ref[...])
    nq = DA_HEADS * 2 * DA_DH // o_ref.shape[1]

    rot = j < 2 * nq
    half = ROPE_DIMS // 2
    w = x.shape[1]
    c = jnp.concatenate([jnp.where(rot, cs_ref[0], 1.0)] * nrep, axis=1)
    s1 = jnp.concatenate([jnp.where(rot, cs_ref[1], 0.0)] * nrep, axis=1)
    s2 = jnp.concatenate([jnp.where(rot, cs_ref[2], 0.0)] * nrep, axis=1)
    y = x * c + pltpu.roll(x, w - half, 1) * s1 + pltpu.roll(x, half, 1) * s2
    o_ref[...] = (y * jnp.where(j < nq, DA_DH ** -0.5 * math.log2(math.e), 1.0)).astype(BF16)


def _qkv_rope(x, g, w, pos, freq):
    t, d = x.shape
    n = w.shape[1]
    tm = min(512, t)
    tn = 4 * MXU_WIDTH
    return pl.pallas_call(
        _qkv_rope_kernel,
        grid=(t // tm, n // tn),
        in_specs=[pl.BlockSpec((tm, d), lambda i, j: (i, 0)),
                  pl.BlockSpec((1, d), lambda i, j: (0, 0)),
                  pl.BlockSpec((d, tn), lambda i, j: (0, j)),
                  pl.BlockSpec((tm, 1), lambda i, j: (i, 0)),
                  pl.BlockSpec((1, LANES), lambda i, j: (0, 0))],
        out_specs=pl.BlockSpec((tm, tn), lambda i, j: (i, j)),
        out_shape=jax.ShapeDtypeStruct((t, n), BF16),
        scratch_shapes=[pltpu.VMEM((tm, d), BF16), pltpu.VMEM((3, tm, LANES), F32)],
        compiler_params=_params(("parallel", "arbitrary")),
        name="qkv_rope",
    )(x, g.reshape(1, d), w, pos, freq)


def _attn_kernel(q_ref, k_ref, v_ref, lq1, lk1, lq2, lk2, sg_ref, o_ref, *, lambda_init):
    lam = (jnp.exp(jnp.sum(lq1[...] * lk1[...], axis=-1, keepdims=True))
           - jnp.exp(jnp.sum(lq2[...] * lk2[...], axis=-1, keepdims=True)) + lambda_init)
    q = q_ref[...]
    k = k_ref[...]
    v = v_ref[...]
    s1 = _dot_nt(q[:, 0:DA_DH], k[:, 0:DA_DH])
    s2 = _dot_nt(q[:, DA_DH:], k[:, DA_DH:])
    e1 = jnp.exp2(s1 - jnp.max(s1, axis=-1, keepdims=True))
    e2 = jnp.exp2(s2 - jnp.max(s2, axis=-1, keepdims=True))
    r1 = 1.0 / jnp.sum(e1, axis=-1, keepdims=True)
    r2 = lam / jnp.sum(e2, axis=-1, keepdims=True)
    o = _dot(e1.astype(BF16), v) * r1 - _dot(e2.astype(BF16), v) * r2
    o_ref[...] = (_rms(o, sg_ref[...]) * (1.0 - lambda_init)).astype(BF16)


def _diff_attention(qkvb, b, s, lq1, lk1, lq2, lk2, subln_g, lambda_init):
    t = b * s
    hw = 2 * DA_DH
    tq = min(512, s)
    nq = s // tq

    def row(n):
        return pl.BlockSpec((1, n), lambda bb, h, i: (0, 0))

    return pl.pallas_call(
        functools.partial(_attn_kernel, lambda_init=lambda_init),
        grid=(b, DA_HEADS, nq),
        in_specs=[pl.BlockSpec((tq, hw), lambda bb, h, i: (bb * nq + i, h)),
                  pl.BlockSpec((s, hw), lambda bb, h, i: (bb, DA_HEADS + h)),
                  pl.BlockSpec((s, hw), lambda bb, h, i: (bb, 2 * DA_HEADS + h)),
                  row(DA_DH), row(DA_DH), row(DA_DH), row(DA_DH), row(hw)],
        out_specs=pl.BlockSpec((tq, hw), lambda bb, h, i: (bb * nq + i, h)),
        out_shape=jax.ShapeDtypeStruct((t, DA_HEADS * hw), BF16),
        compiler_params=_params(("parallel", "parallel", "arbitrary")),
        name="diff_attention",
    )(qkvb, qkvb, qkvb, lq1.reshape(1, -1), lk1.reshape(1, -1), lq2.reshape(1, -1),
      lk2.reshape(1, -1), subln_g.reshape(1, -1))


def _mm_res_kernel(a_ref, w_ref, x_ref, o_ref):
    o_ref[...] = x_ref[...] + _dot(a_ref[...], w_ref[...])


def _matmul_residual(a, w, x):
    t, d = x.shape
    kk = a.shape[1]
    tm = min(512, t)
    tn = 1024
    return pl.pallas_call(
        _mm_res_kernel,
        grid=(t // tm, d // tn),
        in_specs=[pl.BlockSpec((tm, kk), lambda i, j: (i, 0)),
                  pl.BlockSpec((kk, tn), lambda i, j: (0, j)),
                  pl.BlockSpec((tm, tn), lambda i, j: (i, j))],
        out_specs=pl.BlockSpec((tm, tn), lambda i, j: (i, j)),
        out_shape=jax.ShapeDtypeStruct((t, d), F32),
        compiler_params=_params(("parallel", "arbitrary")),
        name="matmul_residual",
    )(a, w, x)


def _odd_mixer(x, b, s, positions, norm_g, w_qkv, lq1, lk1, lq2, lk2, subln_g, w_o, lambda_init):
    t = b * s
    inv_freq = ROPE_THETA ** (-jnp.arange(0, ROPE_DIMS, 2, dtype=F32) / ROPE_DIMS)
    freq = jnp.concatenate([inv_freq, inv_freq, jnp.zeros((LANES - ROPE_DIMS,), F32)]).reshape(1, LANES)
    qkvb = _qkv_rope(x, norm_g, w_qkv.astype(BF16), positions.reshape(t, 1), freq)
    attn = _diff_attention(qkvb, b, s, lq1, lk1, lq2, lk2, subln_g, lambda_init)
    return _matmul_residual(attn, w_o.astype(BF16), x)


def kernel(x, positions, norm_ffn1, ffn1_wg, ffn1_wu, ffn1_wd, norm_mix, norm_ffn2, ffn2_wg, ffn2_wu, ffn2_wd, ev_w_in, ev_conv_w, ev_conv_b, ev_ln_g, ev_ln_b, ev_short_w, ev_a_log, ev_dt_bias, ev_onorm_g, ev_w_out, od_w_qkv, od_lq1, od_lk1, od_lq2, od_lk2, od_subln_g, od_w_o, final_norm):
    b, s, d = x.shape
    depth = norm_ffn1.shape[0]
    xf = x.reshape(b * s, d)
    for i in range(depth):
        xf = _ffn(xf, norm_ffn1[i], ffn1_wg[i], ffn1_wu[i], ffn1_wd[i])
        j = i // 2
        if i % 2 == 0:
            xf = _even_mixer(xf, b, s, norm_mix[i], ev_w_in[j], ev_conv_w[j], ev_conv_b[j], ev_ln_g[j],
                             ev_ln_b[j], ev_short_w[j], ev_a_log[j], ev_dt_bias[j], ev_onorm_g[j],
                             ev_w_out[j])
        else:
            lambda_init = 0.8 - 0.6 * math.exp(-0.3 * i)
            xf = _odd_mixer(xf, b, s, positions, norm_mix[i], od_w_qkv[j], od_lq1[j], od_lk1[j],
                            od_lq2[j], od_lk2[j], od_subln_g[j], od_w_o[j], lambda_init)
        xf = _ffn(xf, norm_ffn2[i], ffn2_wg[i], ffn2_wu[i], ffn2_wd[i],
                  final_g=final_norm if i == depth - 1 else None)
    return xf.reshape(b, s, d)
```

```python
import functools
import math

import jax
import jax.numpy as jnp
from jax import lax
from jax.experimental import pallas as pl
from jax.experimental.pallas import tpu as pltpu

F32 = jnp.float32
BF16 = jnp.bfloat16
EPS = 1e-6

D_FF = 5632
CONV_CH = 1024
CONV_WIDTH = 31
DN_HEADS = 8
DN_DK = 128
DN_DV = 128
SHORT_CONV = 3
CHUNK = 64
DA_HEADS = 8
DA_DH = 128
ROPE_THETA = 500000.0
ROPE_DIMS = DA_DH // 4

LANES = 128
CONV_HALO = 16
SHORT_HALO = 8
MXU_WIDTH = 256
EVEN_IN_PAD = 6400
SCAL_COL_BLOCK = 6144 // LANES
VMEM_LIMIT = 56 * 1024 * 1024
HIGHEST = lax.Precision.HIGHEST


def _sigmoid(x):
    return 1.0 / (1.0 + jnp.exp(-x))


def _silu(x):
    return x * _sigmoid(x)


def _softplus(x):
    return jnp.maximum(x, 0.0) + jnp.log(1.0 + jnp.exp(-jnp.abs(x)))


def _rms(x, g):
    ms = jnp.mean(x * x, axis=-1, keepdims=True)
    return x * lax.rsqrt(ms + EPS) * g


def _dot(a, b):
    return jnp.dot(a, b, preferred_element_type=F32)


def _dot_nt(a, b, precision=None):
    return lax.dot_general(a, b, (((1,), (1,)), ((), ())), precision=precision,
                           preferred_element_type=F32)


def _split3(x):
    hi = x.astype(BF16)
    r1 = x - hi.astype(F32)
    mid = r1.astype(BF16)
    lo = (r1 - mid.astype(F32)).astype(BF16)
    return hi, mid, lo


def _params(sem):
    return pltpu.CompilerParams(dimension_semantics=sem, vmem_limit_bytes=VMEM_LIMIT)


def _ffn_kernel(x_ref, g_ref, wg_ref, wu_ref, wd_ref, *rest, final):
    if final:
        fg_ref, o_ref, xn_ref, acc_ref = rest
    else:
        o_ref, xn_ref, acc_ref = rest
    j = pl.program_id(1)

    @pl.when(j == 0)
    def _():
        xn_ref[...] = _rms(x_ref[...], g_ref[...]).astype(BF16)
        acc_ref[...] = jnp.zeros_like(acc_ref)

    xn = xn_ref[...]
    a = _dot(xn, wg_ref[...])
    b = _dot(xn, wu_ref[...])
    h = (_silu(a) * b).astype(BF16)
    acc_ref[...] += _dot(h, wd_ref[...])

    @pl.when(j == pl.num_programs(1) - 1)
    def _():
        y = x_ref[...] + 0.5 * acc_ref[...]
        if final:
            y = _rms(y, fg_ref[...])
        o_ref[...] = y


def _ffn(x, g, wg, wu, wd, final_g=None):
    t, d = x.shape
    ff = wg.shape[1]
    tm = min(512, t)
    tf = 512
    final = final_g is not None
    in_specs = [
        pl.BlockSpec((tm, d), lambda i, j: (i, 0)),
        pl.BlockSpec((1, d), lambda i, j: (0, 0)),
        pl.BlockSpec((d, tf), lambda i, j: (0, j)),
        pl.BlockSpec((d, tf), lambda i, j: (0, j)),
        pl.BlockSpec((tf, d), lambda i, j: (j, 0)),
    ]
    args = [x, g.reshape(1, d), wg.astype(BF16), wu.astype(BF16), wd.astype(BF16)]
    if final:
        in_specs.append(pl.BlockSpec((1, d), lambda i, j: (0, 0)))
        args.append(final_g.reshape(1, d))
    return pl.pallas_call(
        functools.partial(_ffn_kernel, final=final),
        grid=(t // tm, ff // tf),
        in_specs=in_specs,
        out_specs=pl.BlockSpec((tm, d), lambda i, j: (i, 0)),
        out_shape=jax.ShapeDtypeStruct((t, d), F32),
        scratch_shapes=[pltpu.VMEM((tm, d), BF16), pltpu.VMEM((tm, d), F32)],
        compiler_params=_params(("parallel", "arbitrary")),
        name="ffn",
    )(*args)


def _norm_mm_kernel(x_ref, g_ref, w_ref, o_ref, xn_ref):
    @pl.when(pl.program_id(1) == 0)
    def _():
        xn_ref[...] = _rms(x_ref[...], g_ref[...]).astype(BF16)

    o_ref[...] = _dot(xn_ref[...], w_ref[...])


def _norm_matmul(x, g, w, tn):
    t, d = x.shape
    n = w.shape[1]
    tm = min(1024, t)
    return pl.pallas_call(
        _norm_mm_kernel,
        grid=(t // tm, n // tn),
        in_specs=[
            pl.BlockSpec((tm, d), lambda i, j: (i, 0)),
            pl.BlockSpec((1, d), lambda i, j: (0, 0)),
            pl.BlockSpec((d, tn), lambda i, j: (0, j)),
        ],
        out_specs=pl.BlockSpec((tm, tn), lambda i, j: (i, j)),
        out_shape=jax.ShapeDtypeStruct((t, n), F32),
        scratch_shapes=[pltpu.VMEM((tm, d), BF16)],
        compiler_params=_params(("parallel", "arbitrary")),
        name="norm_matmul",
    )(x, g.reshape(1, d), w)


def _conv_kernel(pv, pg, hvp, hgp, hvn, hgn, w_ref, o_ref, scr, *, ts):
    i = pl.program_id(1)
    last = pl.num_programs(1) - 1
    prev = jnp.where(i > 0, hvp[0] * _sigmoid(hgp[0]), 0.0)
    nxt = jnp.where(i < last, hvn[0] * _sigmoid(hgn[0]), 0.0)
    scr[0:CONV_HALO, :] = prev
    scr[CONV_HALO:CONV_HALO + ts, :] = pv[0] * _sigmoid(pg[0])
    scr[CONV_HALO + ts:2 * CONV_HALO + ts, :] = nxt
    acc = w_ref[0:1, :] * scr[1:1 + ts, :]
    for k in range(1, CONV_WIDTH):
        acc = acc + w_ref[k:k + 1, :] * scr[1 + k:1 + k + ts, :]
    o_ref[0] = acc


def _glu_conv(p3, conv_w, ts):
    b, s, _ = p3.shape
    nc = CONV_CH // LANES
    r = ts // CONV_HALO
    nh = s // CONV_HALO

    def main(off):
        return pl.BlockSpec((1, ts, LANES), lambda bb, i, c: (bb, i, off + c))

    def prev(off):
        return pl.BlockSpec((1, CONV_HALO, LANES),
                            lambda bb, i, c: (bb, jnp.maximum(i * r - 1, 0), off + c))

    def nxt(off):
        return pl.BlockSpec((1, CONV_HALO, LANES),
                            lambda bb, i, c: (bb, jnp.minimum((i + 1) * r, nh - 1), off + c))

    return pl.pallas_call(
        functools.partial(_conv_kernel, ts=ts),
        grid=(b, s // ts, nc),
        in_specs=[main(0), main(nc), prev(0), prev(nc), nxt(0), nxt(nc),
                  pl.BlockSpec((CONV_WIDTH, LANES), lambda bb, i, c: (0, c))],
        out_specs=pl.BlockSpec((1, ts, LANES), lambda bb, i, c: (bb, i, c)),
        out_shape=jax.ShapeDtypeStruct((b, s, CONV_CH), F32),
        scratch_shapes=[pltpu.VMEM((ts + 2 * CONV_HALO, LANES), F32)],
        compiler_params=_params(("parallel", "parallel", "parallel")),
        name="glu_conv",
    )(p3, p3, p3, p3, p3, p3, conv_w)


def _short_conv_kernel(x_ref, hp, hn, w_ref, o_ref, scr, *, ts):
    i = pl.program_id(1)
    c = pl.program_id(2)
    last = pl.num_programs(1) - 1
    scr[0:SHORT_HALO, :] = jnp.where(i > 0, hp[0], 0.0)
    scr[SHORT_HALO:SHORT_HALO + ts, :] = x_ref[0]
    scr[SHORT_HALO + ts:2 * SHORT_HALO + ts, :] = jnp.where(i < last, hn[0], 0.0)
    y = (w_ref[0:1, :] * scr[SHORT_HALO - 1:SHORT_HALO - 1 + ts, :]
         + w_ref[1:2, :] * scr[SHORT_HALO:SHORT_HALO + ts, :]
         + w_ref[2:3, :] * scr[SHORT_HALO + 1:SHORT_HALO + 1 + ts, :])
    y = _silu(y)

    @pl.when(c == 2)
    def _():
        o_ref[0] = y

    @pl.when(c < 2)
    def _():
        scale = jnp.where(c == 0, DN_DK ** -0.5, 1.0)
        for h in range(DN_HEADS):
            sl = slice(h * DN_DK, (h + 1) * DN_DK)
            yh = y[:, sl]
            o_ref[0, :, sl] = yh * (lax.rsqrt(jnp.sum(yh * yh, axis=-1, keepdims=True) + EPS) * scale)


def _short_conv(p3, short_w, ts):
    b, s, _ = p3.shape
    w = DN_HEADS * DN_DK
    nc = short_w.shape[1] // w
    off = 2 * CONV_CH // w
    r = ts // SHORT_HALO
    nh = s // SHORT_HALO
    return pl.pallas_call(
        functools.partial(_short_conv_kernel, ts=ts),
        grid=(b, s // ts, nc),
        in_specs=[
            pl.BlockSpec((1, ts, w), lambda bb, i, c: (bb, i, off + c)),
            pl.BlockSpec((1, SHORT_HALO, w),
                         lambda bb, i, c: (bb, jnp.maximum(i * r - 1, 0), off + c)),
            pl.BlockSpec((1, SHORT_HALO, w),
                         lambda bb, i, c: (bb, jnp.minimum((i + 1) * r, nh - 1), off + c)),
            pl.BlockSpec((SHORT_CONV, w), lambda bb, i, c: (0, c)),
        ],
        out_specs=pl.BlockSpec((1, ts, w), lambda bb, i, c: (bb, i, c)),
        out_shape=jax.ShapeDtypeStruct((b, s, nc * w), F32),
        scratch_shapes=[pltpu.VMEM((ts + 2 * SHORT_HALO, w), F32)],
        compiler_params=_params(("parallel", "parallel", "parallel")),
        name="short_conv",
    )(p3, p3, p3, short_w)


def _delta_prep_kernel(q_ref, k_ref, v_ref, sc_ref, alog_ref, dtb_ref,
                       u_ref, wq_ref, akd_ref, gl_ref):
    sc = sc_ref[0]
    beta = _sigmoid(sc)
    araw = pltpu.roll(sc, LANES - 2 * DN_HEADS, 1)
    g = -jnp.exp(alog_ref[...]) * _softplus(araw + dtb_ref[...])

    ii = lax.broadcasted_iota(jnp.int32, (CHUNK, CHUNK), 0)
    jj = lax.broadcasted_iota(jnp.int32, (CHUNK, CHUNK), 1)
    tri = jnp.concatenate([(ii >= jj).astype(BF16), (ii <= jj).astype(BF16)], axis=0)
    gsum = _dot(tri, jnp.concatenate(_split3(g), axis=1))
    gsum = gsum[:, 0:LANES] + gsum[:, LANES:2 * LANES] + gsum[:, 2 * LANES:]
    gcf = gsum[0:CHUNK]
    gcb = gsum[CHUNK:]
    lane = lax.broadcasted_iota(jnp.int32, (CHUNK, LANES), 1)
    gc = jnp.where(lane < DN_HEADS, gcf, gcb)
    gtot = jnp.broadcast_to(gcf[CHUNK - 1:CHUNK, :], (CHUNK, LANES))
    gexp = jnp.exp(gc)
    kdec = jnp.exp(gtot - gc)
    gl = jnp.exp(gtot)

    r128 = lax.broadcasted_iota(jnp.int32, (LANES, LANES), 0)
    c128 = lax.broadcasted_iota(jnp.int32, (LANES, LANES), 1)
    eye128 = (r128 == c128).astype(F32)

    nch = 2 * DN_HEADS
    wcat = nch * CHUNK
    er = lax.broadcasted_iota(jnp.int32, (LANES, wcat), 0)
    el = lax.broadcasted_iota(jnp.int32, (LANES, wcat), 1)
    expand = (jnp.right_shift(el, 6) == er).astype(BF16)
    bhi, bmid, _ = _split3(beta)
    spread = _dot(jnp.concatenate(_split3(gc) + (bhi, bmid), axis=0), expand)
    gcat = spread[0:CHUNK] + spread[CHUNK:2 * CHUNK] + spread[2 * CHUNK:3 * CHUNK]
    bcat = spread[3 * CHUNK:4 * CHUNK] + spread[4 * CHUNK:]
    tt = lax.broadcasted_iota(jnp.int32, (CHUNK, wcat), 0)
    ll = lax.broadcasted_iota(jnp.int32, (CHUNK, wcat), 1)
    jl = jnp.bitwise_and(ll, CHUNK - 1)
    diag = tt == jl
    rowcat = jnp.broadcast_to(jnp.sum(jnp.where(diag, gcat, 0.0), axis=0, keepdims=True),
                              (CHUNK, wcat))
    ahead = jnp.where(ll < DN_HEADS * CHUNK, tt - jl, jl - tt)
    incl = ahead >= 0
    strict = ahead > 0
    dec = jnp.exp(jnp.where(incl, gcat - rowcat, -jnp.inf))

    kks, qks, ks, qs, vs = [], [], [], [], []
    for h in range(DN_HEADS):
        sl = slice(h * DN_DK, (h + 1) * DN_DK)
        qh = q_ref[0, :, sl]
        kh = k_ref[0, :, sl]
        khb = kh.astype(BF16)
        gram = _dot_nt(jnp.concatenate([khb, qh.astype(BF16)], axis=0), khb)
        kks.append(gram[0:CHUNK])
        qks.append(gram[CHUNK:])
        ks.append(kh)
        qs.append(qh)
        vs.append(v_ref[0, :, sl])
    kkcat = jnp.concatenate(kks + kks, axis=1)
    qkcat = jnp.concatenate(qks + qks, axis=1)
    mcat = jnp.where(strict, -(kkcat * bcat) * dec, 0.0)
    acat = jnp.where(incl, qkcat * dec, 0.0)

    gw = 4 * CHUNK
    br = lax.broadcasted_iota(jnp.int32, (gw, gw), 0)
    bc = lax.broadcasted_iota(jnp.int32, (gw, gw), 1)
    bmask = jnp.right_shift(br, 6) == jnp.right_shift(bc, 6)

    def blockdiag(xg):
        xb = xg.astype(BF16)
        return jnp.where(bmask, jnp.concatenate([xb] * 4, axis=0), jnp.zeros((), BF16))

    ngrp = wcat // gw
    xs = [mcat[:, g * gw:(g + 1) * gw] for g in range(ngrp)]
    tinv = [jnp.where(diag[:, g * gw:(g + 1) * gw], 1.0, 0.0) + xs[g] for g in range(ngrp)]
    xbd = [blockdiag(x) for x in xs]
    for _ in range(5):
        xs = [_dot(xs[g].astype(BF16), xbd[g]) for g in range(ngrp)]
        xbd = [blockdiag(x) for x in xs]
        tinv = [tinv[g] + _dot(tinv[g].astype(BF16), xbd[g]) for g in range(ngrp)]

    eye128b = eye128.astype(BF16)
    rhs, kds, qds = [], [], []
    for c in range(nch):
        h = c % DN_HEADS
        bcol = beta[:, c:c + 1]
        gex = gexp[:, c:c + 1]
        rhs.append(jnp.concatenate([vs[h] * bcol, ks[h] * (bcol * gex)], axis=1).astype(BF16))
        kds.append((ks[h] * kdec[:, c:c + 1]).astype(BF16))
        qds.append(qs[h] * gex)
    uws, kdts = [], []
    for c in range(nch):
        g, o = divmod(c, 4)
        tc = tinv[g][:, o * CHUNK:(o + 1) * CHUNK]
        uws.append(_dot(tc.astype(BF16), rhs[c]))
        kdts.append(_dot_nt(eye128b, kds[c]))
    for c in range(nch):
        u_ref[0, 0, c] = uws[c][:, 0:DN_DV]
        wq_ref[0, 0, c] = jnp.concatenate([uws[c][:, DN_DV:], qds[c]], axis=0).astype(BF16)
        akd_ref[0, 0, c] = jnp.concatenate([acat[:, c * CHUNK:(c + 1) * CHUNK], kdts[c]],
                                           axis=0).astype(BF16)
        gl_ref[0, 0, c] = jnp.broadcast_to(gl[0:8, c:c + 1], (8, LANES))


def _delta_prep(qkvn, p3, a_log, dt_bias):
    b, s, _ = qkvn.shape
    n = s // CHUNK
    nch = 2 * DN_HEADS
    w = DN_HEADS * DN_DK

    def pad_row(a):
        return jnp.pad(a.reshape(1, nch), ((0, 0), (0, LANES - nch)))

    return pl.pallas_call(
        _delta_prep_kernel,
        grid=(b, n),
        in_specs=[
            pl.BlockSpec((1, CHUNK, w), lambda bb, i: (bb, i, 0)),
            pl.BlockSpec((1, CHUNK, w), lambda bb, i: (bb, i, 1)),
            pl.BlockSpec((1, CHUNK, w), lambda bb, i: (bb, i, 2)),
            pl.BlockSpec((1, CHUNK, LANES), lambda bb, i: (bb, i, SCAL_COL_BLOCK)),
            pl.BlockSpec((1, LANES), lambda bb, i: (0, 0)),
            pl.BlockSpec((1, LANES), lambda bb, i: (0, 0)),
        ],
        out_specs=[
            pl.BlockSpec((1, 1, nch, CHUNK, DN_DV), lambda bb, i: (bb, i, 0, 0, 0)),
            pl.BlockSpec((1, 1, nch, 2 * CHUNK, DN_DK), lambda bb, i: (bb, i, 0, 0, 0)),
            pl.BlockSpec((1, 1, nch, CHUNK + DN_DK, CHUNK), lambda bb, i: (bb, i, 0, 0, 0)),
            pl.BlockSpec((1, 1, nch, 8, LANES), lambda bb, i: (bb, i, 0, 0, 0)),
        ],
        out_shape=[
            jax.ShapeDtypeStruct((b, n, nch, CHUNK, DN_DV), F32),
            jax.ShapeDtypeStruct((b, n, nch, 2 * CHUNK, DN_DK), BF16),
            jax.ShapeDtypeStruct((b, n, nch, CHUNK + DN_DK, CHUNK), BF16),
            jax.ShapeDtypeStruct((b, n, nch, 8, LANES), F32),
        ],
        compiler_params=_params(("parallel", "parallel")),
        name="delta_prep",
    )(qkvn, qkvn, qkvn, p3, pad_row(a_log), pad_row(dt_bias))


def _delta_scan_kernel(uf, wqf, akdf, glf, ub, wqb, akdb, glb, of_ref, ob_ref, st_ref, *, cb):
    @pl.when(pl.program_id(1) == 0)
    def _():
        st_ref[...] = jnp.zeros_like(st_ref)

    for step in range(cb):
        for d, (u_r, wq_r, akd_r, gl_r, o_r) in enumerate(((uf, wqf, akdf, glf, of_ref),
                                                           (ub, wqb, akdb, glb, ob_ref))):
            ci = step if d == 0 else cb - 1 - step
            for h in range(DN_HEADS):
                ch = d * DN_HEADS + h
                st = st_ref[ch]
                ws = _dot(wq_r[0, ci, h], st.astype(BF16))
                vnew = u_r[0, ci, h] - ws[0:CHUNK]
                r = _dot(akd_r[0, ci, h], vnew.astype(BF16))
                o_r[0, ci * CHUNK:(ci + 1) * CHUNK, h * DN_DV:(h + 1) * DN_DV] = ws[CHUNK:] + r[0:CHUNK]
                st_ref[ch] = st * gl_r[0, ci, h, 0:1, :] + r[CHUNK:]


def _delta_scan(u, wq, akd, gl, cb=2):
    b, n = u.shape[0], u.shape[1]
    s = n * CHUNK
    nb = n // cb
    hb = DN_HEADS

    def spec(shape, back):
        if back:
            return pl.BlockSpec(shape, lambda bb, i: (bb, nb - 1 - i, 1) + (0,) * (len(shape) - 3))
        return pl.BlockSpec(shape, lambda bb, i: (bb, i, 0) + (0,) * (len(shape) - 3))

    shapes = [(1, cb, hb, CHUNK, DN_DV), (1, cb, hb, 2 * CHUNK, DN_DK),
              (1, cb, hb, CHUNK + DN_DK, CHUNK), (1, cb, hb, 8, LANES)]
    in_specs = [spec(sh, False) for sh in shapes] + [spec(sh, True) for sh in shapes]
    ow = hb * DN_DV
    return pl.pallas_call(
        functools.partial(_delta_scan_kernel, cb=cb),
        grid=(b, nb),
        in_specs=in_specs,
        out_specs=[pl.BlockSpec((1, cb * CHUNK, ow), lambda bb, i: (bb, i, 0)),
                   pl.BlockSpec((1, cb * CHUNK, ow), lambda bb, i: (bb, nb - 1 - i, 0))],
        out_shape=[jax.ShapeDtypeStruct((b, s, ow), F32)] * 2,
        scratch_shapes=[pltpu.VMEM((2 * hb, DN_DK, DN_DV), F32)],
        compiler_params=_params(("parallel", "arbitrary")),
        name="delta_scan",
    )(u, wq, akd, gl, u, wq, akd, gl)


def _even_out_kernel(cv_ref, cb_ref, lg_ref, lb_ref, of_ref, ob_ref, z_ref, og_ref, w_ref, x_ref, o_ref):
    a = cv_ref[...] + cb_ref[...]
    mu = jnp.mean(a, axis=-1, keepdims=True)
    xc = a - mu
    y = xc * lax.rsqrt(jnp.mean(xc * xc, axis=-1, keepdims=True) + EPS)
    acc = x_ref[...] + _dot(_silu(y * lg_ref[...] + lb_ref[...]).astype(BF16), w_ref[0:CONV_CH, :])
    gated = []
    for h in range(DN_HEADS):
        sl = slice(h * DN_DV, (h + 1) * DN_DV)
        o = of_ref[:, sl] + ob_ref[:, sl]
        gated.append((_rms(o, og_ref[...]) * _silu(z_ref[:, sl])).astype(BF16))
    o_ref[...] = acc + _dot(jnp.concatenate(gated, axis=1), w_ref[CONV_CH:, :])


def _even_out(conv, conv_b, ln_g, ln_b, o_f, o_b, p, onorm_g, w_out, x):
    t, d = x.shape
    tm = min(256, t)
    kk = CONV_CH + DN_HEADS * DN_DV
    zblk = (2 * CONV_CH + 3 * DN_HEADS * DN_DK) // CONV_CH

    def row(n):
        return pl.BlockSpec((1, n), lambda i: (0, 0))

    def tile(n, jblk=0):
        return pl.BlockSpec((tm, n), lambda i: (i, jblk))

    return pl.pallas_call(
        _even_out_kernel,
        grid=(t // tm,),
        in_specs=[tile(CONV_CH), row(CONV_CH), row(CONV_CH), row(CONV_CH),
                  tile(CONV_CH), tile(CONV_CH), tile(CONV_CH, zblk), row(DN_DV),
                  pl.BlockSpec((kk, d), lambda i: (0, 0)),
                  tile(d)],
        out_specs=tile(d),
        out_shape=jax.ShapeDtypeStruct((t, d), F32),
        compiler_params=_params(("parallel",)),
        name="even_out",
    )(conv, conv_b.reshape(1, -1), ln_g.reshape(1, -1), ln_b.reshape(1, -1), o_f, o_b, p,
      onorm_g.reshape(1, -1), w_out.astype(BF16), x)


def _even_mixer(x, b, s, norm_g, w_in, conv_w, conv_b, ln_g, ln_b, short_w, a_log, dt_bias,
                onorm_g, w_out):
    t = b * s
    w_pad = jnp.pad(w_in, ((0, 0), (0, EVEN_IN_PAD - w_in.shape[1]))).astype(BF16)
    p = _norm_matmul(x, norm_g, w_pad, tn=5 * MXU_WIDTH)
    p3 = p.reshape(b, s, EVEN_IN_PAD)
    ts = min(512, s)
    conv = _glu_conv(p3, conv_w, ts)
    qkvn = _short_conv(p3, short_w, ts)
    u, wq, akd, gl = _delta_prep(qkvn, p3, a_log, dt_bias)
    o_f, o_b = _delta_scan(u, wq, akd, gl)
    return _even_out(conv.reshape(t, -1), conv_b, ln_g, ln_b, o_f.reshape(t, -1), o_b.reshape(t, -1),
                     p, onorm_g, w_out, x)


def _qkv_rope_kernel(xin_ref, g_ref, w_ref, pos_ref, freq_ref, o_ref, xn_ref, cs_ref):
    j = pl.program_id(1)
    nrep = o_ref.shape[1] // LANES

    @pl.when(j == 0)
    def _():
        xn_ref[...] = _rms(xin_ref[...], g_ref[...]).astype(BF16)
        ang = pos_ref[...].astype(F32) * freq_ref[...]
        lane = lax.broadcasted_iota(jnp.int32, ang.shape, 1)
        half = ROPE_DIMS // 2
        cs_ref[0] = jnp.where(lane < ROPE_DIMS, jnp.cos(ang), 1.0)
        sn = jnp.sin(ang)
        cs_ref[1] = jnp.where(lane < half, -sn, 0.0)
        cs_ref[2] = jnp.where((lane >= half) & (lane < ROPE_DIMS), sn, 0.0)

    x = _dot(xn_ref[...], w_ref[...])
    nq = DA_HEADS * 2 * DA_DH // o_ref.shape[1]

    rot = j < 2 * nq
    half = ROPE_DIMS // 2
    w = x.shape[1]
    qs = jnp.where(j < nq, DA_DH ** -0.5 * math.log2(math.e), 1.0)
    c = jnp.concatenate([jnp.where(rot, cs_ref[0], 1.0) * qs] * nrep, axis=1)
    s1 = jnp.concatenate([jnp.where(rot, cs_ref[1], 0.0) * qs] * nrep, axis=1)
    s2 = jnp.concatenate([jnp.where(rot, cs_ref[2], 0.0) * qs] * nrep, axis=1)
    o_ref[...] = (x * c + pltpu.roll(x, w - half, 1) * s1 + pltpu.roll(x, half, 1) * s2).astype(BF16)


def _qkv_rope(x, g, w, pos, freq):
    t, d = x.shape
    n = w.shape[1]
    tm = min(512, t)
    tn = 4 * MXU_WIDTH
    return pl.pallas_call(
        _qkv_rope_kernel,
        grid=(t // tm, n // tn),
        in_specs=[pl.BlockSpec((tm, d), lambda i, j: (i, 0)),
                  pl.BlockSpec((1, d), lambda i, j: (0, 0)),
                  pl.BlockSpec((d, tn), lambda i, j: (0, j)),
                  pl.BlockSpec((tm, 1), lambda i, j: (i, 0)),
                  pl.BlockSpec((1, LANES), lambda i, j: (0, 0))],
        out_specs=pl.BlockSpec((tm, tn), lambda i, j: (i, j)),
        out_shape=jax.ShapeDtypeStruct((t, n), BF16),
        scratch_shapes=[pltpu.VMEM((tm, d), BF16), pltpu.VMEM((3, tm, LANES), F32)],
        compiler_params=_params(("parallel", "arbitrary")),
        name="qkv_rope",
    )(x, g.reshape(1, d), w, pos, freq)


def _attn_kernel(q_ref, k_ref, v_ref, lq1, lk1, lq2, lk2, sg_ref, o_ref, *, lambda_init, nsplit):
    lam = (jnp.exp(jnp.sum(lq1[...] * lk1[...], axis=-1, keepdims=True))
           - jnp.exp(jnp.sum(lq2[...] * lk2[...], axis=-1, keepdims=True)) + lambda_init)
    k = k_ref[...]
    v = v_ref[...]
    rows = q_ref.shape[0] // nsplit
    for part in range(nsplit):
        rs = slice(part * rows, (part + 1) * rows)
        q = q_ref[rs, :]
        s1 = _dot_nt(q[:, 0:DA_DH], k[:, 0:DA_DH])
        s2 = _dot_nt(q[:, DA_DH:], k[:, DA_DH:])
        e1 = jnp.exp2(s1 - jnp.max(s1, axis=-1, keepdims=True))
        e2 = jnp.exp2(s2 - jnp.max(s2, axis=-1, keepdims=True))
        r1 = 1.0 / jnp.sum(e1, axis=-1, keepdims=True)
        r2 = lam / jnp.sum(e2, axis=-1, keepdims=True)
        o = _dot(e1.astype(BF16), v) * r1 - _dot(e2.astype(BF16), v) * r2
        o_ref[rs, :] = (_rms(o, sg_ref[...]) * (1.0 - lambda_init)).astype(BF16)


def _diff_attention(qkvb, b, s, lq1, lk1, lq2, lk2, subln_g, lambda_init):
    t = b * s
    hw = 2 * DA_DH
    tq = min(512, s)
    nq = s // tq

    def row(n):
        return pl.BlockSpec((1, n), lambda bb, h, i: (0, 0))

    return pl.pallas_call(
        functools.partial(_attn_kernel, lambda_init=lambda_init, nsplit=4 if tq >= 256 else 1),
        grid=(b, DA_HEADS, nq),
        in_specs=[pl.BlockSpec((tq, hw), lambda bb, h, i: (bb * nq + i, h)),
                  pl.BlockSpec((s, hw), lambda bb, h, i: (bb, DA_HEADS + h)),
                  pl.BlockSpec((s, hw), lambda bb, h, i: (bb, 2 * DA_HEADS + h)),
                  row(DA_DH), row(DA_DH), row(DA_DH), row(DA_DH), row(hw)],
        out_specs=pl.BlockSpec((tq, hw), lambda bb, h, i: (bb * nq + i, h)),
        out_shape=jax.ShapeDtypeStruct((t, DA_HEADS * hw), BF16),
        compiler_params=_params(("parallel", "parallel", "arbitrary")),
        name="diff_attention",
    )(qkvb, qkvb, qkvb, lq1.reshape(1, -1), lk1.reshape(1, -1), lq2.reshape(1, -1),
      lk2.reshape(1, -1), subln_g.reshape(1, -1))


def _mm_res_kernel(a_ref, w_ref, x_ref, o_ref):
    o_ref[...] = x_ref[...] + _dot(a_ref[...], w_ref[...])


def _matmul_residual(a, w, x):
    t, d = x.shape
    kk = a.shape[1]
    tm = min(512, t)
    tn = 1024
    return pl.pallas_call(
        _mm_res_kernel,
        grid=(t // tm, d // tn),
        in_specs=[pl.BlockSpec((tm, kk), lambda i, j: (i, 0)),
                  pl.BlockSpec((kk, tn), lambda i, j: (0, j)),
                  pl.BlockSpec((tm, tn), lambda i, j: (i, j))],
        out_specs=pl.BlockSpec((tm, tn), lambda i, j: (i, j)),
        out_shape=jax.ShapeDtypeStruct((t, d), F32),
        compiler_params=_params(("parallel", "arbitrary")),
        name="matmul_residual",
    )(a, w, x)


def _odd_mixer(x, b, s, positions, norm_g, w_qkv, lq1, lk1, lq2, lk2, subln_g, w_o, lambda_init):
    t = b * s
    inv_freq = ROPE_THETA ** (-jnp.arange(0, ROPE_DIMS, 2, dtype=F32) / ROPE_DIMS)
    freq = jnp.concatenate([inv_freq, inv_freq, jnp.zeros((LANES - ROPE_DIMS,), F32)]).reshape(1, LANES)
    qkvb = _qkv_rope(x, norm_g, w_qkv.astype(BF16), positions.reshape(t, 1), freq)
    attn = _diff_attention(qkvb, b, s, lq1, lk1, lq2, lk2, subln_g, lambda_init)
    return _matmul_residual(attn, w_o.astype(BF16), x)


def kernel(x, positions, norm_ffn1, ffn1_wg, ffn1_wu, ffn1_wd, norm_mix, norm_ffn2, ffn2_wg, ffn2_wu, ffn2_wd, ev_w_in, ev_conv_w, ev_conv_b, ev_ln_g, ev_ln_b, ev_short_w, ev_a_log, ev_dt_bias, ev_onorm_g, ev_w_out, od_w_qkv, od_lq1, od_lk1, od_lq2, od_lk2, od_subln_g, od_w_o, final_norm):
    b, s, d = x.shape
    depth = norm_ffn1.shape[0]
    xf = x.reshape(b * s, d)
    for i in range(depth):
        xf = _ffn(xf, norm_ffn1[i], ffn1_wg[i], ffn1_wu[i], ffn1_wd[i])
        j = i // 2
        if i % 2 == 0:
            xf = _even_mixer(xf, b, s, norm_mix[i], ev_w_in[j], ev_conv_w[j], ev_conv_b[j], ev_ln_g[j],
                             ev_ln_b[j], ev_short_w[j], ev_a_log[j], ev_dt_bias[j], ev_onorm_g[j],
                             ev_w_out[j])
        else:
            lambda_init = 0.8 - 0.6 * math.exp(-0.3 * i)
            xf = _odd_mixer(xf, b, s, positions, norm_mix[i], od_w_qkv[j], od_lq1[j], od_lk1[j],
                            od_lq2[j], od_lk2[j], od_subln_g[j], od_w_o[j], lambda_init)
        xf = _ffn(xf, norm_ffn2[i], ffn2_wg[i], ffn2_wu[i], ffn2_wd[i],
                  final_g=final_norm if i == depth - 1 else None)
    return xf.reshape(b, s, d)
```

```python
import functools
import math

import jax
import jax.numpy as jnp
from jax import lax
from jax.experimental import pallas as pl
from jax.experimental.pallas import tpu as pltpu

F32 = jnp.float32
BF16 = jnp.bfloat16
EPS = 1e-6

D_FF = 5632
CONV_CH = 1024
CONV_WIDTH = 31
DN_HEADS = 8
DN_DK = 128
DN_DV = 128
SHORT_CONV = 3
CHUNK = 64
DA_HEADS = 8
DA_DH = 128
ROPE_THETA = 500000.0
ROPE_DIMS = DA_DH // 4

LANES = 128
CONV_HALO = 16
SHORT_HALO = 8
MXU_WIDTH = 256
EVEN_IN_PAD = 6400
SCAL_COL_BLOCK = 6144 // LANES
VMEM_LIMIT = 56 * 1024 * 1024
HIGHEST = lax.Precision.HIGHEST


def _sigmoid(x):
    return 1.0 / (1.0 + jnp.exp(-x))


def _silu(x):
    return x * _sigmoid(x)


def _softplus(x):
    return jnp.maximum(x, 0.0) + jnp.log(1.0 + jnp.exp(-jnp.abs(x)))


def _rms(x, g):
    ms = jnp.mean(x * x, axis=-1, keepdims=True)
    return x * lax.rsqrt(ms + EPS) * g


def _dot(a, b):
    return jnp.dot(a, b, preferred_element_type=F32)


def _dot_nt(a, b, precision=None):
    return lax.dot_general(a, b, (((1,), (1,)), ((), ())), precision=precision,
                           preferred_element_type=F32)


def _split3(x):
    hi = x.astype(BF16)
    r1 = x - hi.astype(F32)
    mid = r1.astype(BF16)
    lo = (r1 - mid.astype(F32)).astype(BF16)
    return hi, mid, lo


def _params(sem):
    return pltpu.CompilerParams(dimension_semantics=sem, vmem_limit_bytes=VMEM_LIMIT)


def _cast_kernel(w_ref, o_ref):
    o_ref[...] = w_ref[...].astype(BF16)


def _cast_bf16(w, layer=None):
    rows, cols = w.shape[-2:]
    nblk = 8
    tr = rows // nblk
    if layer is None:
        in_spec = pl.BlockSpec((tr, cols), lambda r: (r, 0))
    else:
        in_spec = pl.BlockSpec((None, tr, cols), lambda r: (layer, r, 0))
    return pl.pallas_call(
        _cast_kernel,
        grid=(nblk,),
        in_specs=[in_spec],
        out_specs=pl.BlockSpec((tr, cols), lambda r: (r, 0)),
        out_shape=jax.ShapeDtypeStruct((rows, cols), BF16),
        compiler_params=_params(("parallel",)),
        name="cast_bf16",
    )(w)


def _ffn_kernel(x_ref, g_ref, wg_ref, wu_ref, wd_ref, *rest, final):
    if final:
        fg_ref, o_ref, xn_ref, acc_ref = rest
    else:
        o_ref, xn_ref, acc_ref = rest
    j = pl.program_id(1)

    @pl.when(j == 0)
    def _():
        xn_ref[...] = _rms(x_ref[...], g_ref[...]).astype(BF16)
        acc_ref[...] = jnp.zeros_like(acc_ref)

    xn = xn_ref[...]
    a = _dot(xn, wg_ref[...])
    b = _dot(xn, wu_ref[...])
    h = (_silu(a) * b).astype(BF16)
    acc_ref[...] += _dot(h, wd_ref[...])

    @pl.when(j == pl.num_programs(1) - 1)
    def _():
        y = x_ref[...] + 0.5 * acc_ref[...]
        if final:
            y = _rms(y, fg_ref[...])
        o_ref[...] = y


def _ffn(x, g, wg, wu, wd, layer, final_g=None):
    t, d = x.shape
    ff = wg.shape[-1]
    tm = min(512, t)
    tf = 512
    final = final_g is not None
    in_specs = [
        pl.BlockSpec((tm, d), lambda i, j: (i, 0)),
        pl.BlockSpec((1, d), lambda i, j: (0, 0)),
        pl.BlockSpec((d, tf), lambda i, j: (0, j)),
        pl.BlockSpec((d, tf), lambda i, j: (0, j)),
        pl.BlockSpec((tf, d), lambda i, j: (j, 0)),
    ]
    args = [x, g.reshape(1, d), _cast_bf16(wg, layer), _cast_bf16(wu, layer), _cast_bf16(wd, layer)]
    if final:
        in_specs.append(pl.BlockSpec((1, d), lambda i, j: (0, 0)))
        args.append(final_g.reshape(1, d))
    return pl.pallas_call(
        functools.partial(_ffn_kernel, final=final),
        grid=(t // tm, ff // tf),
        in_specs=in_specs,
        out_specs=pl.BlockSpec((tm, d), lambda i, j: (i, 0)),
        out_shape=jax.ShapeDtypeStruct((t, d), F32),
        scratch_shapes=[pltpu.VMEM((tm, d), BF16), pltpu.VMEM((tm, d), F32)],
        compiler_params=_params(("parallel", "arbitrary")),
        name="ffn",
    )(*args)


def _norm_mm_kernel(x_ref, g_ref, w_ref, o_ref, xn_ref):
    @pl.when(pl.program_id(1) == 0)
    def _():
        xn_ref[...] = _rms(x_ref[...], g_ref[...]).astype(BF16)

    o_ref[...] = _dot(xn_ref[...], w_ref[...])


def _norm_matmul(x, g, w, tn):
    t, d = x.shape
    n = w.shape[1]
    tm = min(1024, t)
    return pl.pallas_call(
        _norm_mm_kernel,
        grid=(t // tm, n // tn),
        in_specs=[
            pl.BlockSpec((tm, d), lambda i, j: (i, 0)),
            pl.BlockSpec((1, d), lambda i, j: (0, 0)),
            pl.BlockSpec((d, tn), lambda i, j: (0, j)),
        ],
        out_specs=pl.BlockSpec((tm, tn), lambda i, j: (i, j)),
        out_shape=jax.ShapeDtypeStruct((t, n), F32),
        scratch_shapes=[pltpu.VMEM((tm, d), BF16)],
        compiler_params=_params(("parallel", "arbitrary")),
        name="norm_matmul",
    )(x, g.reshape(1, d), w)


def _conv_kernel(pv, pg, hvp, hgp, hvn, hgn, w_ref, o_ref, scr, *, ts):
    i = pl.program_id(1)
    last = pl.num_programs(1) - 1
    prev = jnp.where(i > 0, hvp[0] * _sigmoid(hgp[0]), 0.0)
    nxt = jnp.where(i < last, hvn[0] * _sigmoid(hgn[0]), 0.0)
    scr[0:CONV_HALO, :] = prev
    scr[CONV_HALO:CONV_HALO + ts, :] = pv[0] * _sigmoid(pg[0])
    scr[CONV_HALO + ts:2 * CONV_HALO + ts, :] = nxt
    acc = w_ref[0:1, :] * scr[1:1 + ts, :]
    for k in range(1, CONV_WIDTH):
        acc = acc + w_ref[k:k + 1, :] * scr[1 + k:1 + k + ts, :]
    o_ref[0] = acc


def _glu_conv(p3, conv_w, ts):
    b, s, _ = p3.shape
    nc = CONV_CH // LANES
    r = ts // CONV_HALO
    nh = s // CONV_HALO

    def main(off):
        return pl.BlockSpec((1, ts, LANES), lambda bb, i, c: (bb, i, off + c))

    def prev(off):
        return pl.BlockSpec((1, CONV_HALO, LANES),
                            lambda bb, i, c: (bb, jnp.maximum(i * r - 1, 0), off + c))

    def nxt(off):
        return pl.BlockSpec((1, CONV_HALO, LANES),
                            lambda bb, i, c: (bb, jnp.minimum((i + 1) * r, nh - 1), off + c))

    return pl.pallas_call(
        functools.partial(_conv_kernel, ts=ts),
        grid=(b, s // ts, nc),
        in_specs=[main(0), main(nc), prev(0), prev(nc), nxt(0), nxt(nc),
                  pl.BlockSpec((CONV_WIDTH, LANES), lambda bb, i, c: (0, c))],
        out_specs=pl.BlockSpec((1, ts, LANES), lambda bb, i, c: (bb, i, c)),
        out_shape=jax.ShapeDtypeStruct((b, s, CONV_CH), F32),
        scratch_shapes=[pltpu.VMEM((ts + 2 * CONV_HALO, LANES), F32)],
        compiler_params=_params(("parallel", "parallel", "parallel")),
        name="glu_conv",
    )(p3, p3, p3, p3, p3, p3, conv_w)


def _short_conv_kernel(x_ref, hp, hn, w_ref, o_ref, scr, *, ts):
    i = pl.program_id(1)
    c = pl.program_id(2)
    last = pl.num_programs(1) - 1
    scr[0:SHORT_HALO, :] = jnp.where(i > 0, hp[0], 0.0)
    scr[SHORT_HALO:SHORT_HALO + ts, :] = x_ref[0]
    scr[SHORT_HALO + ts:2 * SHORT_HALO + ts, :] = jnp.where(i < last, hn[0], 0.0)
    y = (w_ref[0:1, :] * scr[SHORT_HALO - 1:SHORT_HALO - 1 + ts, :]
         + w_ref[1:2, :] * scr[SHORT_HALO:SHORT_HALO + ts, :]
         + w_ref[2:3, :] * scr[SHORT_HALO + 1:SHORT_HALO + 1 + ts, :])
    y = _silu(y)

    @pl.when(c == 2)
    def _():
        o_ref[0] = y

    @pl.when(c < 2)
    def _():
        scale = jnp.where(c == 0, DN_DK ** -0.5, 1.0)
        for h in range(DN_HEADS):
            sl = slice(h * DN_DK, (h + 1) * DN_DK)
            yh = y[:, sl]
            o_ref[0, :, sl] = yh * (lax.rsqrt(jnp.sum(yh * yh, axis=-1, keepdims=True) + EPS) * scale)


def _short_conv(p3, short_w, ts):
    b, s, _ = p3.shape
    w = DN_HEADS * DN_DK
    nc = short_w.shape[1] // w
    off = 2 * CONV_CH // w
    r = ts // SHORT_HALO
    nh = s // SHORT_HALO
    return pl.pallas_call(
        functools.partial(_short_conv_kernel, ts=ts),
        grid=(b, s // ts, nc),
        in_specs=[
            pl.BlockSpec((1, ts, w), lambda bb, i, c: (bb, i, off + c)),
            pl.BlockSpec((1, SHORT_HALO, w),
                         lambda bb, i, c: (bb, jnp.maximum(i * r - 1, 0), off + c)),
            pl.BlockSpec((1, SHORT_HALO, w),
                         lambda bb, i, c: (bb, jnp.minimum((i + 1) * r, nh - 1), off + c)),
            pl.BlockSpec((SHORT_CONV, w), lambda bb, i, c: (0, c)),
        ],
        out_specs=pl.BlockSpec((1, ts, w), lambda bb, i, c: (bb, i, c)),
        out_shape=jax.ShapeDtypeStruct((b, s, nc * w), F32),
        scratch_shapes=[pltpu.VMEM((ts + 2 * SHORT_HALO, w), F32)],
        compiler_params=_params(("parallel", "parallel", "parallel")),
        name="short_conv",
    )(p3, p3, p3, short_w)


def _delta_prep_kernel(q_ref, k_ref, v_ref, sc_ref, alog_ref, dtb_ref,
                       u_ref, wq_ref, akd_ref, gl_ref):
    sc = sc_ref[0]
    beta = _sigmoid(sc)
    araw = pltpu.roll(sc, LANES - 2 * DN_HEADS, 1)
    g = -jnp.exp(alog_ref[...]) * _softplus(araw + dtb_ref[...])

    ii = lax.broadcasted_iota(jnp.int32, (CHUNK, CHUNK), 0)
    jj = lax.broadcasted_iota(jnp.int32, (CHUNK, CHUNK), 1)
    tri = jnp.concatenate([(ii >= jj).astype(BF16), (ii <= jj).astype(BF16)], axis=0)
    gsum = _dot(tri, jnp.concatenate(_split3(g), axis=1))
    gsum = gsum[:, 0:LANES] + gsum[:, LANES:2 * LANES] + gsum[:, 2 * LANES:]
    gcf = gsum[0:CHUNK]
    gcb = gsum[CHUNK:]
    lane = lax.broadcasted_iota(jnp.int32, (CHUNK, LANES), 1)
    gc = jnp.where(lane < DN_HEADS, gcf, gcb)
    gtot = jnp.broadcast_to(gcf[CHUNK - 1:CHUNK, :], (CHUNK, LANES))
    gexp = jnp.exp(gc)
    kdec = jnp.exp(gtot - gc)
    gl = jnp.exp(gtot)

    r128 = lax.broadcasted_iota(jnp.int32, (LANES, LANES), 0)
    c128 = lax.broadcasted_iota(jnp.int32, (LANES, LANES), 1)
    eye128 = (r128 == c128).astype(F32)

    nch = 2 * DN_HEADS
    wcat = nch * CHUNK
    er = lax.broadcasted_iota(jnp.int32, (LANES, wcat), 0)
    el = lax.broadcasted_iota(jnp.int32, (LANES, wcat), 1)
    expand = (jnp.right_shift(el, 6) == er).astype(BF16)
    bhi, bmid, _ = _split3(beta)
    spread = _dot(jnp.concatenate(_split3(gc) + (bhi, bmid), axis=0), expand)
    gcat = spread[0:CHUNK] + spread[CHUNK:2 * CHUNK] + spread[2 * CHUNK:3 * CHUNK]
    bcat = spread[3 * CHUNK:4 * CHUNK] + spread[4 * CHUNK:]
    tt = lax.broadcasted_iota(jnp.int32, (CHUNK, wcat), 0)
    ll = lax.broadcasted_iota(jnp.int32, (CHUNK, wcat), 1)
    jl = jnp.bitwise_and(ll, CHUNK - 1)
    diag = tt == jl
    rowcat = jnp.broadcast_to(jnp.sum(jnp.where(diag, gcat, 0.0), axis=0, keepdims=True),
                              (CHUNK, wcat))
    ahead = jnp.where(ll < DN_HEADS * CHUNK, tt - jl, jl - tt)
    incl = ahead >= 0
    strict = ahead > 0
    dec = jnp.exp(jnp.where(incl, gcat - rowcat, -jnp.inf))

    kks, qks, ks, qs, vs = [], [], [], [], []
    for h in range(DN_HEADS):
        sl = slice(h * DN_DK, (h + 1) * DN_DK)
        qh = q_ref[0, :, sl]
        kh = k_ref[0, :, sl]
        khb = kh.astype(BF16)
        gram = _dot_nt(jnp.concatenate([khb, qh.astype(BF16)], axis=0), khb)
        kks.append(gram[0:CHUNK])
        qks.append(gram[CHUNK:])
        ks.append(kh)
        qs.append(qh)
        vs.append(v_ref[0, :, sl])
    kkcat = jnp.concatenate(kks + kks, axis=1)
    qkcat = jnp.concatenate(qks + qks, axis=1)
    mcat = jnp.where(strict, -(kkcat * bcat) * dec, 0.0)
    acat = jnp.where(incl, qkcat * dec, 0.0)

    gw = 4 * CHUNK
    br = lax.broadcasted_iota(jnp.int32, (gw, gw), 0)
    bc = lax.broadcasted_iota(jnp.int32, (gw, gw), 1)
    bmask = jnp.right_shift(br, 6) == jnp.right_shift(bc, 6)

    def blockdiag(xg):
        xb = xg.astype(BF16)
        return jnp.where(bmask, jnp.concatenate([xb] * 4, axis=0), jnp.zeros((), BF16))

    ngrp = wcat // gw
    xs = [mcat[:, g * gw:(g + 1) * gw] for g in range(ngrp)]
    tinv = [jnp.where(diag[:, g * gw:(g + 1) * gw], 1.0, 0.0) + xs[g] for g in range(ngrp)]
    xbd = [blockdiag(x) for x in xs]
    for _ in range(5):
        xs = [_dot(xs[g].astype(BF16), xbd[g]) for g in range(ngrp)]
        xbd = [blockdiag(x) for x in xs]
        tinv = [tinv[g] + _dot(tinv[g].astype(BF16), xbd[g]) for g in range(ngrp)]

    eye128b = eye128.astype(BF16)
    rhs, kds, qds = [], [], []
    for c in range(nch):
        h = c % DN_HEADS
        bcol = beta[:, c:c + 1]
        gex = gexp[:, c:c + 1]
        rhs.append(jnp.concatenate([vs[h] * bcol, ks[h] * (bcol * gex)], axis=1).astype(BF16))
        kds.append((ks[h] * kdec[:, c:c + 1]).astype(BF16))
        qds.append(qs[h] * gex)
    uws, kdts = [], []
    for c in range(nch):
        g, o = divmod(c, 4)
        tc = tinv[g][:, o * CHUNK:(o + 1) * CHUNK]
        uws.append(_dot(tc.astype(BF16), rhs[c]))
        kdts.append(_dot_nt(eye128b, kds[c]))
    for c in range(nch):
        u_ref[0, 0, c] = uws[c][:, 0:DN_DV]
        wq_ref[0, 0, c] = jnp.concatenate([uws[c][:, DN_DV:], qds[c]], axis=0).astype(BF16)
        akd_ref[0, 0, c] = jnp.concatenate([acat[:, c * CHUNK:(c + 1) * CHUNK], kdts[c]],
                                           axis=0).astype(BF16)
        gl_ref[0, 0, c] = jnp.broadcast_to(gl[0:8, c:c + 1], (8, LANES))


def _delta_prep(qkvn, p3, a_log, dt_bias):
    b, s, _ = qkvn.shape
    n = s // CHUNK
    nch = 2 * DN_HEADS
    w = DN_HEADS * DN_DK

    def pad_row(a):
        return jnp.pad(a.reshape(1, nch), ((0, 0), (0, LANES - nch)))

    return pl.pallas_call(
        _delta_prep_kernel,
        grid=(b, n),
        in_specs=[
            pl.BlockSpec((1, CHUNK, w), lambda bb, i: (bb, i, 0)),
            pl.BlockSpec((1, CHUNK, w), lambda bb, i: (bb, i, 1)),
            pl.BlockSpec((1, CHUNK, w), lambda bb, i: (bb, i, 2)),
            pl.BlockSpec((1, CHUNK, LANES), lambda bb, i: (bb, i, SCAL_COL_BLOCK)),
            pl.BlockSpec((1, LANES), lambda bb, i: (0, 0)),
            pl.BlockSpec((1, LANES), lambda bb, i: (0, 0)),
        ],
        out_specs=[
            pl.BlockSpec((1, 1, nch, CHUNK, DN_DV), lambda bb, i: (bb, i, 0, 0, 0)),
            pl.BlockSpec((1, 1, nch, 2 * CHUNK, DN_DK), lambda bb, i: (bb, i, 0, 0, 0)),
            pl.BlockSpec((1, 1, nch, CHUNK + DN_DK, CHUNK), lambda bb, i: (bb, i, 0, 0, 0)),
            pl.BlockSpec((1, 1, nch, 8, LANES), lambda bb, i: (bb, i, 0, 0, 0)),
        ],
        out_shape=[
            jax.ShapeDtypeStruct((b, n, nch, CHUNK, DN_DV), F32),
            jax.ShapeDtypeStruct((b, n, nch, 2 * CHUNK, DN_DK), BF16),
            jax.ShapeDtypeStruct((b, n, nch, CHUNK + DN_DK, CHUNK), BF16),
            jax.ShapeDtypeStruct((b, n, nch, 8, LANES), F32),
        ],
        compiler_params=_params(("parallel", "parallel")),
        name="delta_prep",
    )(qkvn, qkvn, qkvn, p3, pad_row(a_log), pad_row(dt_bias))


def _delta_scan_kernel(uf, wqf, akdf, glf, ub, wqb, akdb, glb, of_ref, ob_ref, st_ref, *, cb):
    @pl.when(pl.program_id(1) == 0)
    def _():
        st_ref[...] = jnp.zeros_like(st_ref)

    for step in range(cb):
        for d, (u_r, wq_r, akd_r, gl_r, o_r) in enumerate(((uf, wqf, akdf, glf, of_ref),
                                                           (ub, wqb, akdb, glb, ob_ref))):
            ci = step if d == 0 else cb - 1 - step
            for h in range(DN_HEADS):
                ch = d * DN_HEADS + h
                st = st_ref[ch]
                ws = _dot(wq_r[0, ci, h], st.astype(BF16))
                vnew = u_r[0, ci, h] - ws[0:CHUNK]
                r = _dot(akd_r[0, ci, h], vnew.astype(BF16))
                o_r[0, ci * CHUNK:(ci + 1) * CHUNK, h * DN_DV:(h + 1) * DN_DV] = ws[CHUNK:] + r[0:CHUNK]
                st_ref[ch] = st * gl_r[0, ci, h, 0:1, :] + r[CHUNK:]


def _delta_scan(u, wq, akd, gl, cb=2):
    b, n = u.shape[0], u.shape[1]
    s = n * CHUNK
    nb = n // cb
    hb = DN_HEADS

    def spec(shape, back):
        if back:
            return pl.BlockSpec(shape, lambda bb, i: (bb, nb - 1 - i, 1) + (0,) * (len(shape) - 3))
        return pl.BlockSpec(shape, lambda bb, i: (bb, i, 0) + (0,) * (len(shape) - 3))

    shapes = [(1, cb, hb, CHUNK, DN_DV), (1, cb, hb, 2 * CHUNK, DN_DK),
              (1, cb, hb, CHUNK + DN_DK, CHUNK), (1, cb, hb, 8, LANES)]
    in_specs = [spec(sh, False) for sh in shapes] + [spec(sh, True) for sh in shapes]
    ow = hb * DN_DV
    return pl.pallas_call(
        functools.partial(_delta_scan_kernel, cb=cb),
        grid=(b, nb),
        in_specs=in_specs,
        out_specs=[pl.BlockSpec((1, cb * CHUNK, ow), lambda bb, i: (bb, i, 0)),
                   pl.BlockSpec((1, cb * CHUNK, ow), lambda bb, i: (bb, nb - 1 - i, 0))],
        out_shape=[jax.ShapeDtypeStruct((b, s, ow), F32)] * 2,
        scratch_shapes=[pltpu.VMEM((2 * hb, DN_DK, DN_DV), F32)],
        compiler_params=_params(("parallel", "arbitrary")),
        name="delta_scan",
    )(u, wq, akd, gl, u, wq, akd, gl)


def _even_out_kernel(cv_ref, cb_ref, lg_ref, lb_ref, of_ref, ob_ref, z_ref, og_ref, w_ref, x_ref, o_ref):
    a = cv_ref[...] + cb_ref[...]
    mu = jnp.mean(a, axis=-1, keepdims=True)
    xc = a - mu
    y = xc * lax.rsqrt(jnp.mean(xc * xc, axis=-1, keepdims=True) + EPS)
    acc = x_ref[...] + _dot(_silu(y * lg_ref[...] + lb_ref[...]).astype(BF16), w_ref[0:CONV_CH, :])
    gated = []
    for h in range(DN_HEADS):
        sl = slice(h * DN_DV, (h + 1) * DN_DV)
        o = of_ref[:, sl] + ob_ref[:, sl]
        gated.append((_rms(o, og_ref[...]) * _silu(z_ref[:, sl])).astype(BF16))
    o_ref[...] = acc + _dot(jnp.concatenate(gated, axis=1), w_ref[CONV_CH:, :])


def _even_out(conv, conv_b, ln_g, ln_b, o_f, o_b, p, onorm_g, w_out, x):
    t, d = x.shape
    tm = min(256, t)
    kk = CONV_CH + DN_HEADS * DN_DV
    zblk = (2 * CONV_CH + 3 * DN_HEADS * DN_DK) // CONV_CH

    def row(n):
        return pl.BlockSpec((1, n), lambda i: (0, 0))

    def tile(n, jblk=0):
        return pl.BlockSpec((tm, n), lambda i: (i, jblk))

    return pl.pallas_call(
        _even_out_kernel,
        grid=(t // tm,),
        in_specs=[tile(CONV_CH), row(CONV_CH), row(CONV_CH), row(CONV_CH),
                  tile(CONV_CH), tile(CONV_CH), tile(CONV_CH, zblk), row(DN_DV),
                  pl.BlockSpec((kk, d), lambda i: (0, 0)),
                  tile(d)],
        out_specs=tile(d),
        out_shape=jax.ShapeDtypeStruct((t, d), F32),
        compiler_params=_params(("parallel",)),
        name="even_out",
    )(conv, conv_b.reshape(1, -1), ln_g.reshape(1, -1), ln_b.reshape(1, -1), o_f, o_b, p,
      onorm_g.reshape(1, -1), w_out, x)


def _even_mixer(x, b, s, norm_g, w_in, conv_w, conv_b, ln_g, ln_b, short_w, a_log, dt_bias,
                onorm_g, w_out):
    t = b * s
    w_pad = jnp.pad(w_in, ((0, 0), (0, EVEN_IN_PAD - w_in.shape[1]))).astype(BF16)
    p = _norm_matmul(x, norm_g, w_pad, tn=5 * MXU_WIDTH)
    p3 = p.reshape(b, s, EVEN_IN_PAD)
    ts = min(512, s)
    conv = _glu_conv(p3, conv_w, ts)
    qkvn = _short_conv(p3, short_w, ts)
    u, wq, akd, gl = _delta_prep(qkvn, p3, a_log, dt_bias)
    o_f, o_b = _delta_scan(u, wq, akd, gl)
    return _even_out(conv.reshape(t, -1), conv_b, ln_g, ln_b, o_f.reshape(t, -1), o_b.reshape(t, -1),
                     p, onorm_g, w_out, x)


def _qkv_rope_kernel(xin_ref, g_ref, w_ref, pos_ref, freq_ref, o_ref, xn_ref, cs_ref):
    j = pl.program_id(1)
    nrep = o_ref.shape[1] // LANES

    @pl.when(j == 0)
    def _():
        xn_ref[...] = _rms(xin_ref[...], g_ref[...]).astype(BF16)
        ang = pos_ref[...].astype(F32) * freq_ref[...]
        lane = lax.broadcasted_iota(jnp.int32, ang.shape, 1)
        half = ROPE_DIMS // 2
        cs_ref[0] = jnp.where(lane < ROPE_DIMS, jnp.cos(ang), 1.0)
        sn = jnp.sin(ang)
        cs_ref[1] = jnp.where(lane < half, -sn, 0.0)
        cs_ref[2] = jnp.where((lane >= half) & (lane < ROPE_DIMS), sn, 0.0)

    nq = DA_HEADS * 2 * DA_DH // o_ref.shape[1]

    rot = j < 2 * nq
    half = ROPE_DIMS // 2
    w = o_ref.shape[1] // 2
    qs = jnp.where(j < nq, DA_DH ** -0.5 * math.log2(math.e), 1.0)
    c = jnp.concatenate([jnp.where(rot, cs_ref[0], 1.0) * qs] * (nrep // 2), axis=1)
    s1 = jnp.concatenate([jnp.where(rot, cs_ref[1], 0.0) * qs] * (nrep // 2), axis=1)
    s2 = jnp.concatenate([jnp.where(rot, cs_ref[2], 0.0) * qs] * (nrep // 2), axis=1)
    xn = xn_ref[...]
    for part in range(2):
        x = _dot(xn, w_ref[:, part * w:(part + 1) * w])
        o_ref[:, part * w:(part + 1) * w] = (
            x * c + pltpu.roll(x, w - half, 1) * s1 + pltpu.roll(x, half, 1) * s2).astype(BF16)


def _qkv_rope(x, g, w, pos, freq):
    t, d = x.shape
    n = w.shape[1]
    tm = min(512, t)
    tn = 4 * MXU_WIDTH
    return pl.pallas_call(
        _qkv_rope_kernel,
        grid=(t // tm, n // tn),
        in_specs=[pl.BlockSpec((tm, d), lambda i, j: (i, 0)),
                  pl.BlockSpec((1, d), lambda i, j: (0, 0)),
                  pl.BlockSpec((d, tn), lambda i, j: (0, j)),
                  pl.BlockSpec((tm, 1), lambda i, j: (i, 0)),
                  pl.BlockSpec((1, LANES), lambda i, j: (0, 0))],
        out_specs=pl.BlockSpec((tm, tn), lambda i, j: (i, j)),
        out_shape=jax.ShapeDtypeStruct((t, n), BF16),
        scratch_shapes=[pltpu.VMEM((tm, d), BF16), pltpu.VMEM((3, tm, LANES), F32)],
        compiler_params=_params(("parallel", "arbitrary")),
        name="qkv_rope",
    )(x, g.reshape(1, d), w, pos, freq)


def _attn_kernel(q_ref, k_ref, v_ref, lq1, lk1, lq2, lk2, sg_ref, o_ref, *, lambda_init, nsplit):
    lam = (jnp.exp(jnp.sum(lq1[...] * lk1[...], axis=-1, keepdims=True))
           - jnp.exp(jnp.sum(lq2[...] * lk2[...], axis=-1, keepdims=True)) + lambda_init)
    k = k_ref[...]
    v = v_ref[...]
    rows = q_ref.shape[0] // nsplit
    for part in range(nsplit):
        rs = slice(part * rows, (part + 1) * rows)
        q = q_ref[rs, :]
        s1 = _dot_nt(q[:, 0:DA_DH], k[:, 0:DA_DH])
        s2 = _dot_nt(q[:, DA_DH:], k[:, DA_DH:])
        e1 = jnp.exp2(s1 - jnp.max(s1, axis=-1, keepdims=True))
        e2 = jnp.exp2(s2 - jnp.max(s2, axis=-1, keepdims=True))
        r1 = 1.0 / jnp.sum(e1, axis=-1, keepdims=True)
        r2 = lam / jnp.sum(e2, axis=-1, keepdims=True)
        o = _dot(e1.astype(BF16), v) * r1 - _dot(e2.astype(BF16), v) * r2
        o_ref[rs, :] = (_rms(o, sg_ref[...]) * (1.0 - lambda_init)).astype(BF16)


def _diff_attention(qkvb, b, s, lq1, lk1, lq2, lk2, subln_g, lambda_init):
    t = b * s
    hw = 2 * DA_DH
    tq = min(1024, s)
    nq = s // tq

    def row(n):
        return pl.BlockSpec((1, n), lambda bb, h, i: (0, 0))

    return pl.pallas_call(
        functools.partial(_attn_kernel, lambda_init=lambda_init, nsplit=max(1, tq // 128)),
        grid=(b, DA_HEADS, nq),
        in_specs=[pl.BlockSpec((tq, hw), lambda bb, h, i: (bb * nq + i, h)),
                  pl.BlockSpec((s, hw), lambda bb, h, i: (bb, DA_HEADS + h)),
                  pl.BlockSpec((s, hw), lambda bb, h, i: (bb, 2 * DA_HEADS + h)),
                  row(DA_DH), row(DA_DH), row(DA_DH), row(DA_DH), row(hw)],
        out_specs=pl.BlockSpec((tq, hw), lambda bb, h, i: (bb * nq + i, h)),
        out_shape=jax.ShapeDtypeStruct((t, DA_HEADS * hw), BF16),
        compiler_params=_params(("parallel", "parallel", "arbitrary")),
        name="diff_attention",
    )(qkvb, qkvb, qkvb, lq1.reshape(1, -1), lk1.reshape(1, -1), lq2.reshape(1, -1),
      lk2.reshape(1, -1), subln_g.reshape(1, -1))


def _mm_res_kernel(a_ref, w_ref, x_ref, o_ref):
    o_ref[...] = x_ref[...] + _dot(a_ref[...], w_ref[...])


def _matmul_residual(a, w, x):
    t, d = x.shape
    kk = a.shape[1]
    tm = min(512, t)
    tn = 1024
    return pl.pallas_call(
        _mm_res_kernel,
        grid=(t // tm, d // tn),
        in_specs=[pl.BlockSpec((tm, kk), lambda i, j: (i, 0)),
                  pl.BlockSpec((kk, tn), lambda i, j: (0, j)),
                  pl.BlockSpec((tm, tn), lambda i, j: (i, j))],
        out_specs=pl.BlockSpec((tm, tn), lambda i, j: (i, j)),
        out_shape=jax.ShapeDtypeStruct((t, d), F32),
        compiler_params=_params(("parallel", "arbitrary")),
        name="matmul_residual",
    )(a, w, x)


def _odd_mixer(x, b, s, positions, norm_g, w_qkv, lq1, lk1, lq2, lk2, subln_g, w_o, lambda_init):
    t = b * s
    inv_freq = ROPE_THETA ** (-jnp.arange(0, ROPE_DIMS, 2, dtype=F32) / ROPE_DIMS)
    freq = jnp.concatenate([inv_freq, inv_freq, jnp.zeros((LANES - ROPE_DIMS,), F32)]).reshape(1, LANES)
    qkvb = _qkv_rope(x, norm_g, w_qkv, positions.reshape(t, 1), freq)
    attn = _diff_attention(qkvb, b, s, lq1, lk1, lq2, lk2, subln_g, lambda_init)
    return _matmul_residual(attn, w_o, x)


def kernel(x, positions, norm_ffn1, ffn1_wg, ffn1_wu, ffn1_wd, norm_mix, norm_ffn2, ffn2_wg, ffn2_wu, ffn2_wd, ev_w_in, ev_conv_w, ev_conv_b, ev_ln_g, ev_ln_b, ev_short_w, ev_a_log, ev_dt_bias, ev_onorm_g, ev_w_out, od_w_qkv, od_lq1, od_lk1, od_lq2, od_lk2, od_subln_g, od_w_o, final_norm):
    b, s, d = x.shape
    depth = norm_ffn1.shape[0]
    xf = x.reshape(b * s, d)
    for i in range(depth):
        xf = _ffn(xf, norm_ffn1[i], ffn1_wg, ffn1_wu, ffn1_wd, i)
        j = i // 2
        if i % 2 == 0:
            xf = _even_mixer(xf, b, s, norm_mix[i], ev_w_in[j], ev_conv_w[j], ev_conv_b[j], ev_ln_g[j],
                             ev_ln_b[j], ev_short_w[j], ev_a_log[j], ev_dt_bias[j], ev_onorm_g[j],
                             _cast_bf16(ev_w_out, j))
        else:
            lambda_init = 0.8 - 0.6 * math.exp(-0.3 * i)
            xf = _odd_mixer(xf, b, s, positions, norm_mix[i], _cast_bf16(od_w_qkv, j), od_lq1[j], od_lk1[j],
                            od_lq2[j], od_lk2[j], od_subln_g[j], _cast_bf16(od_w_o, j), lambda_init)
        xf = _ffn(xf, norm_ffn2[i], ffn2_wg, ffn2_wu, ffn2_wd, i,
                  final_g=final_norm if i == depth - 1 else None)
    return xf.reshape(b, s, d)
```

```python
import functools
import math

import jax
import jax.numpy as jnp
from jax import lax
from jax.experimental import pallas as pl
from jax.experimental.pallas import tpu as pltpu

F32 = jnp.float32
BF16 = jnp.bfloat16
EPS = 1e-6

D_FF = 5632
CONV_CH = 1024
CONV_WIDTH = 31
DN_HEADS = 8
DN_DK = 128
DN_DV = 128
SHORT_CONV = 3
CHUNK = 64
DA_HEADS = 8
DA_DH = 128
ROPE_THETA = 500000.0
ROPE_DIMS = DA_DH // 4

LANES = 128
CONV_HALO = 16
SHORT_HALO = 8
MXU_WIDTH = 256
EVEN_IN_PAD = 6400
SCAL_COL_BLOCK = 6144 // LANES
VMEM_LIMIT = 56 * 1024 * 1024
HIGHEST = lax.Precision.HIGHEST


def _sigmoid(x):
    return 1.0 / (1.0 + jnp.exp(-x))


def _silu(x):
    return x * _sigmoid(x)


def _softplus(x):
    return jnp.maximum(x, 0.0) + jnp.log(1.0 + jnp.exp(-jnp.abs(x)))


def _rms(x, g):
    ms = jnp.mean(x * x, axis=-1, keepdims=True)
    return x * lax.rsqrt(ms + EPS) * g


def _dot(a, b):
    return jnp.dot(a, b, preferred_element_type=F32)


def _dot_nt(a, b, precision=None):
    return lax.dot_general(a, b, (((1,), (1,)), ((), ())), precision=precision,
                           preferred_element_type=F32)


def _split3(x):
    hi = x.astype(BF16)
    r1 = x - hi.astype(F32)
    mid = r1.astype(BF16)
    lo = (r1 - mid.astype(F32)).astype(BF16)
    return hi, mid, lo


def _params(sem):
    return pltpu.CompilerParams(dimension_semantics=sem, vmem_limit_bytes=VMEM_LIMIT)


def _cast_kernel(w_ref, o_ref):
    o_ref[...] = w_ref[...].astype(BF16)


def _cast_bf16(w, layer=None):
    rows, cols = w.shape[-2:]
    nblk = 8
    tr = rows // nblk
    if layer is None:
        in_spec = pl.BlockSpec((tr, cols), lambda r: (r, 0))
    else:
        in_spec = pl.BlockSpec((None, tr, cols), lambda r: (layer, r, 0))
    return pl.pallas_call(
        _cast_kernel,
        grid=(nblk,),
        in_specs=[in_spec],
        out_specs=pl.BlockSpec((tr, cols), lambda r: (r, 0)),
        out_shape=jax.ShapeDtypeStruct((rows, cols), BF16),
        compiler_params=_params(("parallel",)),
        name="cast_bf16",
    )(w)


def _ffn_kernel(x_ref, g_ref, wg_ref, wu_ref, wd_ref, *rest, final):
    if final:
        fg_ref, o_ref, xn_ref, acc_ref = rest
    else:
        o_ref, xn_ref, acc_ref = rest
    j = pl.program_id(1)

    @pl.when(j == 0)
    def _():
        xn_ref[...] = _rms(x_ref[...], g_ref[...]).astype(BF16)
        acc_ref[...] = jnp.zeros_like(acc_ref)

    xn = xn_ref[...]
    a = _dot(xn, wg_ref[...])
    b = _dot(xn, wu_ref[...])
    h = (_silu(a) * b).astype(BF16)
    acc_ref[...] += _dot(h, wd_ref[...])

    @pl.when(j == pl.num_programs(1) - 1)
    def _():
        y = x_ref[...] + 0.5 * acc_ref[...]
        if final:
            y = _rms(y, fg_ref[...])
        o_ref[...] = y


def _ffn(x, g, wg, wu, wd, layer, final_g=None):
    t, d = x.shape
    ff = wg.shape[-1]
    tm = min(512, t)
    tf = 512
    final = final_g is not None
    in_specs = [
        pl.BlockSpec((tm, d), lambda i, j: (i, 0)),
        pl.BlockSpec((1, d), lambda i, j: (0, 0)),
        pl.BlockSpec((d, tf), lambda i, j: (0, j)),
        pl.BlockSpec((d, tf), lambda i, j: (0, j)),
        pl.BlockSpec((tf, d), lambda i, j: (j, 0)),
    ]
    args = [x, g.reshape(1, d), _cast_bf16(wg, layer), _cast_bf16(wu, layer), _cast_bf16(wd, layer)]
    if final:
        in_specs.append(pl.BlockSpec((1, d), lambda i, j: (0, 0)))
        args.append(final_g.reshape(1, d))
    return pl.pallas_call(
        functools.partial(_ffn_kernel, final=final),
        grid=(t // tm, ff // tf),
        in_specs=in_specs,
        out_specs=pl.BlockSpec((tm, d), lambda i, j: (i, 0)),
        out_shape=jax.ShapeDtypeStruct((t, d), F32),
        scratch_shapes=[pltpu.VMEM((tm, d), BF16), pltpu.VMEM((tm, d), F32)],
        compiler_params=_params(("parallel", "arbitrary")),
        name="ffn",
    )(*args)


def _norm_mm_kernel(x_ref, g_ref, w_ref, o_ref, xn_ref):
    @pl.when(pl.program_id(1) == 0)
    def _():
        xn_ref[...] = _rms(x_ref[...], g_ref[...]).astype(BF16)

    o_ref[...] = _dot(xn_ref[...], w_ref[...])


def _norm_matmul(x, g, w, tn):
    t, d = x.shape
    n = w.shape[1]
    tm = min(1024, t)
    return pl.pallas_call(
        _norm_mm_kernel,
        grid=(t // tm, n // tn),
        in_specs=[
            pl.BlockSpec((tm, d), lambda i, j: (i, 0)),
            pl.BlockSpec((1, d), lambda i, j: (0, 0)),
            pl.BlockSpec((d, tn), lambda i, j: (0, j)),
        ],
        out_specs=pl.BlockSpec((tm, tn), lambda i, j: (i, j)),
        out_shape=jax.ShapeDtypeStruct((t, n), F32),
        scratch_shapes=[pltpu.VMEM((tm, d), BF16)],
        compiler_params=_params(("parallel", "arbitrary")),
        name="norm_matmul",
    )(x, g.reshape(1, d), w)


def _conv_kernel(pv, pg, hvp, hgp, hvn, hgn, w_ref, o_ref, scr, *, ts):
    i = pl.program_id(1)
    last = pl.num_programs(1) - 1
    prev = jnp.where(i > 0, hvp[0] * _sigmoid(hgp[0]), 0.0)
    nxt = jnp.where(i < last, hvn[0] * _sigmoid(hgn[0]), 0.0)
    scr[0:CONV_HALO, :] = prev
    scr[CONV_HALO:CONV_HALO + ts, :] = pv[0] * _sigmoid(pg[0])
    scr[CONV_HALO + ts:2 * CONV_HALO + ts, :] = nxt
    acc = w_ref[0:1, :] * scr[1:1 + ts, :]
    for k in range(1, CONV_WIDTH):
        acc = acc + w_ref[k:k + 1, :] * scr[1 + k:1 + k + ts, :]
    o_ref[0] = acc


def _glu_conv(p3, conv_w, ts):
    b, s, _ = p3.shape
    nc = CONV_CH // LANES
    r = ts // CONV_HALO
    nh = s // CONV_HALO

    def main(off):
        return pl.BlockSpec((1, ts, LANES), lambda bb, i, c: (bb, i, off + c))

    def prev(off):
        return pl.BlockSpec((1, CONV_HALO, LANES),
                            lambda bb, i, c: (bb, jnp.maximum(i * r - 1, 0), off + c))

    def nxt(off):
        return pl.BlockSpec((1, CONV_HALO, LANES),
                            lambda bb, i, c: (bb, jnp.minimum((i + 1) * r, nh - 1), off + c))

    return pl.pallas_call(
        functools.partial(_conv_kernel, ts=ts),
        grid=(b, s // ts, nc),
        in_specs=[main(0), main(nc), prev(0), prev(nc), nxt(0), nxt(nc),
                  pl.BlockSpec((CONV_WIDTH, LANES), lambda bb, i, c: (0, c))],
        out_specs=pl.BlockSpec((1, ts, LANES), lambda bb, i, c: (bb, i, c)),
        out_shape=jax.ShapeDtypeStruct((b, s, CONV_CH), F32),
        scratch_shapes=[pltpu.VMEM((ts + 2 * CONV_HALO, LANES), F32)],
        compiler_params=_params(("parallel", "parallel", "parallel")),
        name="glu_conv",
    )(p3, p3, p3, p3, p3, p3, conv_w)


def _short_conv_kernel(x_ref, hp, hn, w_ref, o_ref, scr, *, ts):
    i = pl.program_id(1)
    c = pl.program_id(2)
    last = pl.num_programs(1) - 1
    scr[0:SHORT_HALO, :] = jnp.where(i > 0, hp[0], 0.0)
    scr[SHORT_HALO:SHORT_HALO + ts, :] = x_ref[0]
    scr[SHORT_HALO + ts:2 * SHORT_HALO + ts, :] = jnp.where(i < last, hn[0], 0.0)
    y = (w_ref[0:1, :] * scr[SHORT_HALO - 1:SHORT_HALO - 1 + ts, :]
         + w_ref[1:2, :] * scr[SHORT_HALO:SHORT_HALO + ts, :]
         + w_ref[2:3, :] * scr[SHORT_HALO + 1:SHORT_HALO + 1 + ts, :])
    y = _silu(y)

    @pl.when(c == 2)
    def _():
        o_ref[0] = y

    @pl.when(c < 2)
    def _():
        scale = jnp.where(c == 0, DN_DK ** -0.5, 1.0)
        for h in range(DN_HEADS):
            sl = slice(h * DN_DK, (h + 1) * DN_DK)
            yh = y[:, sl]
            o_ref[0, :, sl] = yh * (lax.rsqrt(jnp.sum(yh * yh, axis=-1, keepdims=True) + EPS) * scale)


def _short_conv(p3, short_w, ts):
    b, s, _ = p3.shape
    w = DN_HEADS * DN_DK
    nc = short_w.shape[1] // w
    off = 2 * CONV_CH // w
    r = ts // SHORT_HALO
    nh = s // SHORT_HALO
    return pl.pallas_call(
        functools.partial(_short_conv_kernel, ts=ts),
        grid=(b, s // ts, nc),
        in_specs=[
            pl.BlockSpec((1, ts, w), lambda bb, i, c: (bb, i, off + c)),
            pl.BlockSpec((1, SHORT_HALO, w),
                         lambda bb, i, c: (bb, jnp.maximum(i * r - 1, 0), off + c)),
            pl.BlockSpec((1, SHORT_HALO, w),
                         lambda bb, i, c: (bb, jnp.minimum((i + 1) * r, nh - 1), off + c)),
            pl.BlockSpec((SHORT_CONV, w), lambda bb, i, c: (0, c)),
        ],
        out_specs=pl.BlockSpec((1, ts, w), lambda bb, i, c: (bb, i, c)),
        out_shape=jax.ShapeDtypeStruct((b, s, nc * w), F32),
        scratch_shapes=[pltpu.VMEM((ts + 2 * SHORT_HALO, w), F32)],
        compiler_params=_params(("parallel", "parallel", "parallel")),
        name="short_conv",
    )(p3, p3, p3, short_w)


def _delta_prep_kernel(q_ref, k_ref, v_ref, sc_ref, alog_ref, dtb_ref, pr_ref, qo_ref, gl_ref):
    sc = sc_ref[0]
    beta = _sigmoid(sc)
    araw = pltpu.roll(sc, LANES - 2 * DN_HEADS, 1)
    g = -jnp.exp(alog_ref[...]) * _softplus(araw + dtb_ref[...])

    ii = lax.broadcasted_iota(jnp.int32, (CHUNK, CHUNK), 0)
    jj = lax.broadcasted_iota(jnp.int32, (CHUNK, CHUNK), 1)
    tri = jnp.concatenate([(ii >= jj).astype(BF16), (ii <= jj).astype(BF16)], axis=0)
    gsum = _dot(tri, jnp.concatenate(_split3(g), axis=1))
    gsum = gsum[:, 0:LANES] + gsum[:, LANES:2 * LANES] + gsum[:, 2 * LANES:]
    gcf = gsum[0:CHUNK]
    gcb = gsum[CHUNK:]
    lane = lax.broadcasted_iota(jnp.int32, (CHUNK, LANES), 1)
    gc = jnp.where(lane < DN_HEADS, gcf, gcb)
    gtot = jnp.broadcast_to(gcf[CHUNK - 1:CHUNK, :], (CHUNK, LANES))
    gexp = jnp.exp(gc)
    kdec = jnp.exp(gtot - gc)
    gl = jnp.exp(gtot)

    nch = 2 * DN_HEADS
    wcat = nch * CHUNK
    er = lax.broadcasted_iota(jnp.int32, (LANES, wcat), 0)
    el = lax.broadcasted_iota(jnp.int32, (LANES, wcat), 1)
    expand = (jnp.right_shift(el, 6) == er).astype(BF16)
    bhi, bmid, _ = _split3(beta)
    spread = _dot(jnp.concatenate(_split3(gc) + (bhi, bmid), axis=0), expand)
    gcat = spread[0:CHUNK] + spread[CHUNK:2 * CHUNK] + spread[2 * CHUNK:3 * CHUNK]
    bcat = spread[3 * CHUNK:4 * CHUNK] + spread[4 * CHUNK:]
    tt = lax.broadcasted_iota(jnp.int32, (CHUNK, wcat), 0)
    ll = lax.broadcasted_iota(jnp.int32, (CHUNK, wcat), 1)
    jl = jnp.bitwise_and(ll, CHUNK - 1)
    diag = tt == jl
    rowcat = jnp.broadcast_to(jnp.sum(jnp.where(diag, gcat, 0.0), axis=0, keepdims=True),
                              (CHUNK, wcat))
    ahead = jnp.where(ll < DN_HEADS * CHUNK, tt - jl, jl - tt)
    incl = ahead >= 0
    strict = ahead > 0
    dec = jnp.exp(jnp.where(incl, gcat - rowcat, -jnp.inf))

    kks, qks, ks, qs, vs = [], [], [], [], []
    for h in range(DN_HEADS):
        sl = slice(h * DN_DK, (h + 1) * DN_DK)
        qh = q_ref[0, :, sl]
        kh = k_ref[0, :, sl]
        khb = kh.astype(BF16)
        gram = _dot_nt(jnp.concatenate([khb, qh.astype(BF16)], axis=0), khb)
        kks.append(gram[0:CHUNK])
        qks.append(gram[CHUNK:])
        ks.append(kh)
        qs.append(qh)
        vs.append(v_ref[0, :, sl])
    kkcat = jnp.concatenate(kks + kks, axis=1)
    qkcat = jnp.concatenate(qks + qks, axis=1)
    mcat = jnp.where(strict, -(kkcat * bcat) * dec, 0.0)
    acat = jnp.where(incl, qkcat * dec, 0.0)

    gw = 4 * CHUNK
    br = lax.broadcasted_iota(jnp.int32, (gw, gw), 0)
    bc = lax.broadcasted_iota(jnp.int32, (gw, gw), 1)
    bmask = jnp.right_shift(br, 6) == jnp.right_shift(bc, 6)

    def blockdiag(xg):
        xb = xg.astype(BF16)
        return jnp.where(bmask, jnp.concatenate([xb] * 4, axis=0), jnp.zeros((), BF16))

    ngrp = wcat // gw
    xs = [mcat[:, g * gw:(g + 1) * gw] for g in range(ngrp)]
    tinv = [jnp.where(diag[:, g * gw:(g + 1) * gw], 1.0, 0.0) + xs[g] for g in range(ngrp)]
    xbd = [blockdiag(x) for x in xs]
    for _ in range(5):
        xs = [_dot(xs[g].astype(BF16), xbd[g]) for g in range(ngrp)]
        xbd = [blockdiag(x) for x in xs]
        tinv = [tinv[g] + _dot(tinv[g].astype(BF16), xbd[g]) for g in range(ngrp)]

    rhs, kds, qds = [], [], []
    for c in range(nch):
        h = c % DN_HEADS
        bcol = beta[:, c:c + 1]
        gex = gexp[:, c:c + 1]
        rhs.append(jnp.concatenate([vs[h] * bcol, ks[h] * (bcol * gex)], axis=1).astype(BF16))
        kds.append((ks[h] * kdec[:, c:c + 1]).astype(BF16))
        qds.append(qs[h] * gex)
    uws = []
    for c in range(nch):
        g, o = divmod(c, 4)
        tc = tinv[g][:, o * CHUNK:(o + 1) * CHUNK]
        uws.append(_dot(tc.astype(BF16), rhs[c]).astype(BF16))
    auw = [_dot(acat[:, c * CHUNK:(c + 1) * CHUNK].astype(BF16), uws[c]) for c in range(nch)]
    kuw = [lax.dot_general(kds[c], uws[c], (((0,), (0,)), ((), ())), preferred_element_type=F32)
           for c in range(nch)]
    for c in range(nch):
        pr_ref[0, 0, c] = jnp.concatenate([qds[c] - auw[c][:, DN_DV:], -kuw[c][:, DN_DV:]],
                                          axis=0).astype(BF16)
        qo_ref[0, 0, c] = jnp.concatenate([auw[c][:, 0:DN_DV], kuw[c][:, 0:DN_DV]], axis=0).astype(BF16)
        gl_ref[0, 0, c] = jnp.broadcast_to(gl[0:8, c:c + 1], (8, LANES))


def _delta_prep(qkvn, p3, a_log, dt_bias):
    b, s, _ = qkvn.shape
    n = s // CHUNK
    nch = 2 * DN_HEADS
    w = DN_HEADS * DN_DK

    def pad_row(a):
        return jnp.pad(a.reshape(1, nch), ((0, 0), (0, LANES - nch)))

    return pl.pallas_call(
        _delta_prep_kernel,
        grid=(b, n),
        in_specs=[
            pl.BlockSpec((1, CHUNK, w), lambda bb, i: (bb, i, 0)),
            pl.BlockSpec((1, CHUNK, w), lambda bb, i: (bb, i, 1)),
            pl.BlockSpec((1, CHUNK, w), lambda bb, i: (bb, i, 2)),
            pl.BlockSpec((1, CHUNK, LANES), lambda bb, i: (bb, i, SCAL_COL_BLOCK)),
            pl.BlockSpec((1, LANES), lambda bb, i: (0, 0)),
            pl.BlockSpec((1, LANES), lambda bb, i: (0, 0)),
        ],
        out_specs=[
            pl.BlockSpec((1, 1, nch, CHUNK + DN_DK, DN_DV), lambda bb, i: (bb, i, 0, 0, 0)),
            pl.BlockSpec((1, 1, nch, CHUNK + DN_DK, DN_DV), lambda bb, i: (bb, i, 0, 0, 0)),
            pl.BlockSpec((1, 1, nch, 8, LANES), lambda bb, i: (bb, i, 0, 0, 0)),
        ],
        out_shape=[
            jax.ShapeDtypeStruct((b, n, nch, CHUNK + DN_DK, DN_DV), BF16),
            jax.ShapeDtypeStruct((b, n, nch, CHUNK + DN_DK, DN_DV), BF16),
            jax.ShapeDtypeStruct((b, n, nch, 8, LANES), F32),
        ],
        compiler_params=_params(("parallel", "parallel")),
        name="delta_prep",
    )(qkvn, qkvn, qkvn, p3, pad_row(a_log), pad_row(dt_bias))


def _delta_scan_kernel(prf, qof, glf, prb, qob, glb, of_ref, ob_ref, st_ref, *, cb):
    @pl.when(pl.program_id(1) == 0)
    def _():
        st_ref[...] = jnp.zeros_like(st_ref)

    for step in range(cb):
        for d, (pr_r, qo_r, gl_r, o_r) in enumerate(((prf, qof, glf, of_ref), (prb, qob, glb, ob_ref))):
            ci = step if d == 0 else cb - 1 - step
            for h in range(DN_HEADS):
                ch = d * DN_HEADS + h
                st = st_ref[ch]
                r = _dot(pr_r[0, ci, h], st.astype(BF16)) + qo_r[0, ci, h]
                o_r[0, ci * CHUNK:(ci + 1) * CHUNK, h * DN_DV:(h + 1) * DN_DV] = r[0:CHUNK]
                st_ref[ch] = st * gl_r[0, ci, h, 0:1, :] + r[CHUNK:]


def _delta_scan(pr, qo, gl, cb=2):
    b, n = pr.shape[0], pr.shape[1]
    s = n * CHUNK
    nb = n // cb
    hb = DN_HEADS

    def spec(shape, back):
        if back:
            return pl.BlockSpec(shape, lambda bb, i: (bb, nb - 1 - i, 1) + (0,) * (len(shape) - 3))
        return pl.BlockSpec(shape, lambda bb, i: (bb, i, 0) + (0,) * (len(shape) - 3))

    shapes = [(1, cb, hb, CHUNK + DN_DK, DN_DV), (1, cb, hb, CHUNK + DN_DK, DN_DV), (1, cb, hb, 8, LANES)]
    in_specs = [spec(sh, False) for sh in shapes] + [spec(sh, True) for sh in shapes]
    ow = hb * DN_DV
    return pl.pallas_call(
        functools.partial(_delta_scan_kernel, cb=cb),
        grid=(b, nb),
        in_specs=in_specs,
        out_specs=[pl.BlockSpec((1, cb * CHUNK, ow), lambda bb, i: (bb, i, 0)),
                   pl.BlockSpec((1, cb * CHUNK, ow), lambda bb, i: (bb, nb - 1 - i, 0))],
        out_shape=[jax.ShapeDtypeStruct((b, s, ow), F32)] * 2,
        scratch_shapes=[pltpu.VMEM((2 * hb, DN_DK, DN_DV), F32)],
        compiler_params=_params(("parallel", "arbitrary")),
        name="delta_scan",
    )(pr, qo, gl, pr, qo, gl)


def _even_out_kernel(cv_ref, cb_ref, lg_ref, lb_ref, of_ref, ob_ref, z_ref, og_ref, w_ref, x_ref, o_ref):
    a = cv_ref[...] + cb_ref[...]
    mu = jnp.mean(a, axis=-1, keepdims=True)
    xc = a - mu
    y = xc * lax.rsqrt(jnp.mean(xc * xc, axis=-1, keepdims=True) + EPS)
    acc = x_ref[...] + _dot(_silu(y * lg_ref[...] + lb_ref[...]).astype(BF16), w_ref[0:CONV_CH, :])
    gated = []
    for h in range(DN_HEADS):
        sl = slice(h * DN_DV, (h + 1) * DN_DV)
        o = of_ref[:, sl] + ob_ref[:, sl]
        gated.append((_rms(o, og_ref[...]) * _silu(z_ref[:, sl])).astype(BF16))
    o_ref[...] = acc + _dot(jnp.concatenate(gated, axis=1), w_ref[CONV_CH:, :])


def _even_out(conv, conv_b, ln_g, ln_b, o_f, o_b, p, onorm_g, w_out, x):
    t, d = x.shape
    tm = min(256, t)
    kk = CONV_CH + DN_HEADS * DN_DV
    zblk = (2 * CONV_CH + 3 * DN_HEADS * DN_DK) // CONV_CH

    def row(n):
        return pl.BlockSpec((1, n), lambda i: (0, 0))

    def tile(n, jblk=0):
        return pl.BlockSpec((tm, n), lambda i: (i, jblk))

    return pl.pallas_call(
        _even_out_kernel,
        grid=(t // tm,),
        in_specs=[tile(CONV_CH), row(CONV_CH), row(CONV_CH), row(CONV_CH),
                  tile(CONV_CH), tile(CONV_CH), tile(CONV_CH, zblk), row(DN_DV),
                  pl.BlockSpec((kk, d), lambda i: (0, 0)),
                  tile(d)],
        out_specs=tile(d),
        out_shape=jax.ShapeDtypeStruct((t, d), F32),
        compiler_params=_params(("parallel",)),
        name="even_out",
    )(conv, conv_b.reshape(1, -1), ln_g.reshape(1, -1), ln_b.reshape(1, -1), o_f, o_b, p,
      onorm_g.reshape(1, -1), w_out, x)


def _even_mixer(x, b, s, norm_g, w_in, conv_w, conv_b, ln_g, ln_b, short_w, a_log, dt_bias,
                onorm_g, w_out):
    t = b * s
    w_pad = jnp.pad(w_in, ((0, 0), (0, EVEN_IN_PAD - w_in.shape[1]))).astype(BF16)
    p = _norm_matmul(x, norm_g, w_pad, tn=5 * MXU_WIDTH)
    p3 = p.reshape(b, s, EVEN_IN_PAD)
    ts = min(512, s)
    conv = _glu_conv(p3, conv_w, ts)
    qkvn = _short_conv(p3, short_w, ts)
    pr, qo, gl = _delta_prep(qkvn, p3, a_log, dt_bias)
    o_f, o_b = _delta_scan(pr, qo, gl)
    return _even_out(conv.reshape(t, -1), conv_b, ln_g, ln_b, o_f.reshape(t, -1), o_b.reshape(t, -1),
                     p, onorm_g, w_out, x)


def _qkv_rope_kernel(xin_ref, g_ref, w_ref, pos_ref, freq_ref, o_ref, xn_ref, cs_ref):
    j = pl.program_id(1)
    nrep = o_ref.shape[1] // LANES

    @pl.when(j == 0)
    def _():
        xn_ref[...] = _rms(xin_ref[...], g_ref[...]).astype(BF16)
        ang = pos_ref[...].astype(F32) * freq_ref[...]
        lane = lax.broadcasted_iota(jnp.int32, ang.shape, 1)
        half = ROPE_DIMS // 2
        cs_ref[0] = jnp.where(lane < ROPE_DIMS, jnp.cos(ang), 1.0)
        sn = jnp.sin(ang)
        cs_ref[1] = jnp.where(lane < half, -sn, 0.0)
        cs_ref[2] = jnp.where((lane >= half) & (lane < ROPE_DIMS), sn, 0.0)

    nq = DA_HEADS * 2 * DA_DH // o_ref.shape[1]

    rot = j < 2 * nq
    half = ROPE_DIMS // 2
    w = o_ref.shape[1] // 2
    qs = jnp.where(j < nq, DA_DH ** -0.5 * math.log2(math.e), 1.0)
    c = jnp.concatenate([jnp.where(rot, cs_ref[0], 1.0) * qs] * (nrep // 2), axis=1)
    s1 = jnp.concatenate([jnp.where(rot, cs_ref[1], 0.0) * qs] * (nrep // 2), axis=1)
    s2 = jnp.concatenate([jnp.where(rot, cs_ref[2], 0.0) * qs] * (nrep // 2), axis=1)
    xn = xn_ref[...]
    for part in range(2):
        x = _dot(xn, w_ref[:, part * w:(part + 1) * w])
        o_ref[:, part * w:(part + 1) * w] = (
            x * c + pltpu.roll(x, w - half, 1) * s1 + pltpu.roll(x, half, 1) * s2).astype(BF16)


def _qkv_rope(x, g, w, pos, freq):
    t, d = x.shape
    n = w.shape[1]
    tm = min(512, t)
    tn = 4 * MXU_WIDTH
    return pl.pallas_call(
        _qkv_rope_kernel,
        grid=(t // tm, n // tn),
        in_specs=[pl.BlockSpec((tm, d), lambda i, j: (i, 0)),
                  pl.BlockSpec((1, d), lambda i, j: (0, 0)),
                  pl.BlockSpec((d, tn), lambda i, j: (0, j)),
                  pl.BlockSpec((tm, 1), lambda i, j: (i, 0)),
                  pl.BlockSpec((1, LANES), lambda i, j: (0, 0))],
        out_specs=pl.BlockSpec((tm, tn), lambda i, j: (i, j)),
        out_shape=jax.ShapeDtypeStruct((t, n), BF16),
        scratch_shapes=[pltpu.VMEM((tm, d), BF16), pltpu.VMEM((3, tm, LANES), F32)],
        compiler_params=_params(("parallel", "arbitrary")),
        name="qkv_rope",
    )(x, g.reshape(1, d), w, pos, freq)


def _attn_kernel(q_ref, k_ref, v_ref, lq1, lk1, lq2, lk2, sg_ref, o_ref, *, lambda_init, nsplit):
    lam = (jnp.exp(jnp.sum(lq1[...] * lk1[...], axis=-1, keepdims=True))
           - jnp.exp(jnp.sum(lq2[...] * lk2[...], axis=-1, keepdims=True)) + lambda_init)
    k = k_ref[...]
    v = v_ref[...]
    rows = q_ref.shape[0] // nsplit
    for part in range(nsplit):
        rs = slice(part * rows, (part + 1) * rows)
        q = q_ref[rs, :]
        s1 = _dot_nt(q[:, 0:DA_DH], k[:, 0:DA_DH])
        s2 = _dot_nt(q[:, DA_DH:], k[:, DA_DH:])
        e1 = jnp.exp2(s1 - jnp.max(s1, axis=-1, keepdims=True))
        e2 = jnp.exp2(s2 - jnp.max(s2, axis=-1, keepdims=True))
        r1 = 1.0 / jnp.sum(e1, axis=-1, keepdims=True)
        r2 = lam / jnp.sum(e2, axis=-1, keepdims=True)
        o = _dot(e1.astype(BF16), v) * r1 - _dot(e2.astype(BF16), v) * r2
        o_ref[rs, :] = (_rms(o, sg_ref[...]) * (1.0 - lambda_init)).astype(BF16)


def _diff_attention(qkvb, b, s, lq1, lk1, lq2, lk2, subln_g, lambda_init):
    t = b * s
    hw = 2 * DA_DH
    tq = min(1024, s)
    nq = s // tq

    def row(n):
        return pl.BlockSpec((1, n), lambda bb, h, i: (0, 0))

    return pl.pallas_call(
        functools.partial(_attn_kernel, lambda_init=lambda_init, nsplit=max(1, tq // 128)),
        grid=(b, DA_HEADS, nq),
        in_specs=[pl.BlockSpec((tq, hw), lambda bb, h, i: (bb * nq + i, h)),
                  pl.BlockSpec((s, hw), lambda bb, h, i: (bb, DA_HEADS + h)),
                  pl.BlockSpec((s, hw), lambda bb, h, i: (bb, 2 * DA_HEADS + h)),
                  row(DA_DH), row(DA_DH), row(DA_DH), row(DA_DH), row(hw)],
        out_specs=pl.BlockSpec((tq, hw), lambda bb, h, i: (bb * nq + i, h)),
        out_shape=jax.ShapeDtypeStruct((t, DA_HEADS * hw), BF16),
        compiler_params=_params(("parallel", "parallel", "arbitrary")),
        name="diff_attention",
    )(qkvb, qkvb, qkvb, lq1.reshape(1, -1), lk1.reshape(1, -1), lq2.reshape(1, -1),
      lk2.reshape(1, -1), subln_g.reshape(1, -1))


def _mm_res_kernel(a_ref, w_ref, x_ref, o_ref):
    o_ref[...] = x_ref[...] + _dot(a_ref[...], w_ref[...])


def _matmul_residual(a, w, x):
    t, d = x.shape
    kk = a.shape[1]
    tm = min(512, t)
    tn = 1024
    return pl.pallas_call(
        _mm_res_kernel,
        grid=(t // tm, d // tn),
        in_specs=[pl.BlockSpec((tm, kk), lambda i, j: (i, 0)),
                  pl.BlockSpec((kk, tn), lambda i, j: (0, j)),
                  pl.BlockSpec((tm, tn), lambda i, j: (i, j))],
        out_specs=pl.BlockSpec((tm, tn), lambda i, j: (i, j)),
        out_shape=jax.ShapeDtypeStruct((t, d), F32),
        compiler_params=_params(("parallel", "arbitrary")),
        name="matmul_residual",
    )(a, w, x)


def _odd_mixer(x, b, s, positions, norm_g, w_qkv, lq1, lk1, lq2, lk2, subln_g, w_o, lambda_init):
    t = b * s
    inv_freq = ROPE_THETA ** (-jnp.arange(0, ROPE_DIMS, 2, dtype=F32) / ROPE_DIMS)
    freq = jnp.concatenate([inv_freq, inv_freq, jnp.zeros((LANES - ROPE_DIMS,), F32)]).reshape(1, LANES)
    qkvb = _qkv_rope(x, norm_g, w_qkv, positions.reshape(t, 1), freq)
    attn = _diff_attention(qkvb, b, s, lq1, lk1, lq2, lk2, subln_g, lambda_init)
    return _matmul_residual(attn, w_o, x)


def kernel(x, positions, norm_ffn1, ffn1_wg, ffn1_wu, ffn1_wd, norm_mix, norm_ffn2, ffn2_wg, ffn2_wu, ffn2_wd, ev_w_in, ev_conv_w, ev_conv_b, ev_ln_g, ev_ln_b, ev_short_w, ev_a_log, ev_dt_bias, ev_onorm_g, ev_w_out, od_w_qkv, od_lq1, od_lk1, od_lq2, od_lk2, od_subln_g, od_w_o, final_norm):
    b, s, d = x.shape
    depth = norm_ffn1.shape[0]
    xf = x.reshape(b * s, d)
    for i in range(depth):
        xf = _ffn(xf, norm_ffn1[i], ffn1_wg, ffn1_wu, ffn1_wd, i)
        j = i // 2
        if i % 2 == 0:
            xf = _even_mixer(xf, b, s, norm_mix[i], ev_w_in[j], ev_conv_w[j], ev_conv_b[j], ev_ln_g[j],
                             ev_ln_b[j], ev_short_w[j], ev_a_log[j], ev_dt_bias[j], ev_onorm_g[j],
                             _cast_bf16(ev_w_out, j))
        else:
            lambda_init = 0.8 - 0.6 * math.exp(-0.3 * i)
            xf = _odd_mixer(xf, b, s, positions, norm_mix[i], _cast_bf16(od_w_qkv, j), od_lq1[j], od_lk1[j],
                            od_lq2[j], od_lk2[j], od_subln_g[j], _cast_bf16(od_w_o, j), lambda_init)
        xf = _ffn(xf, norm_ffn2[i], ffn2_wg, ffn2_wu, ffn2_wd, i,
                  final_g=final_norm if i == depth - 1 else None)
    return xf.reshape(b, s, d)
```

```python
import functools
import math

import jax
import jax.numpy as jnp
from jax import lax
from jax.experimental import pallas as pl
from jax.experimental.pallas import tpu as pltpu

F32 = jnp.float32
BF16 = jnp.bfloat16
EPS = 1e-6

D_FF = 5632
CONV_CH = 1024
CONV_WIDTH = 31
DN_HEADS = 8
DN_DK = 128
DN_DV = 128
SHORT_CONV = 3
CHUNK = 64
DA_HEADS = 8
DA_DH = 128
ROPE_THETA = 500000.0
ROPE_DIMS = DA_DH // 4

LANES = 128
CONV_HALO = 16
SHORT_HALO = 8
MXU_WIDTH = 256
EVEN_IN_PAD = 6400
SCAL_COL_BLOCK = 6144 // LANES
VMEM_LIMIT = 56 * 1024 * 1024
HIGHEST = lax.Precision.HIGHEST


def _sigmoid(x):
    return 1.0 / (1.0 + jnp.exp(-x))


def _silu(x):
    return x * _sigmoid(x)


def _softplus(x):
    return jnp.maximum(x, 0.0) + jnp.log(1.0 + jnp.exp(-jnp.abs(x)))


def _rms(x, g):
    ms = jnp.mean(x * x, axis=-1, keepdims=True)
    return x * lax.rsqrt(ms + EPS) * g


def _dot(a, b):
    return jnp.dot(a, b, preferred_element_type=F32)


def _dot_nt(a, b, precision=None):
    return lax.dot_general(a, b, (((1,), (1,)), ((), ())), precision=precision,
                           preferred_element_type=F32)


def _split3(x):
    hi = x.astype(BF16)
    r1 = x - hi.astype(F32)
    mid = r1.astype(BF16)
    lo = (r1 - mid.astype(F32)).astype(BF16)
    return hi, mid, lo


def _params(sem):
    return pltpu.CompilerParams(dimension_semantics=sem, vmem_limit_bytes=VMEM_LIMIT)


def _cast_kernel(w_ref, o_ref):
    cols = w_ref.shape[1]
    o_ref[:, 0:cols] = w_ref[...].astype(BF16)
    if o_ref.shape[1] > cols:
        o_ref[:, cols:] = jnp.zeros((o_ref.shape[0], o_ref.shape[1] - cols), BF16)


def _cast_bf16(w, layer, pad_cols=None):
    rows, cols = w.shape[-2:]
    ocols = cols if pad_cols is None else pad_cols
    nblk = 8
    tr = rows // nblk
    return pl.pallas_call(
        _cast_kernel,
        grid=(nblk,),
        in_specs=[pl.BlockSpec((None, tr, cols), lambda r: (layer, r, 0))],
        out_specs=pl.BlockSpec((tr, ocols), lambda r: (r, 0)),
        out_shape=jax.ShapeDtypeStruct((rows, ocols), BF16),
        compiler_params=_params(("parallel",)),
        name="cast_bf16",
    )(w)


def _ffn_kernel(x_ref, g_ref, wg_ref, wu_ref, wd_ref, *rest, final):
    if final:
        fg_ref, o_ref, xn_ref, acc_ref = rest
    else:
        o_ref, xn_ref, acc_ref = rest
    j = pl.program_id(1)

    @pl.when(j == 0)
    def _():
        xn_ref[...] = _rms(x_ref[...], g_ref[...]).astype(BF16)
        acc_ref[...] = jnp.zeros_like(acc_ref)

    xn = xn_ref[...]
    a = _dot(xn, wg_ref[...])
    b = _dot(xn, wu_ref[...])
    h = (_silu(a) * b).astype(BF16)
    acc_ref[...] += _dot(h, wd_ref[...])

    @pl.when(j == pl.num_programs(1) - 1)
    def _():
        y = x_ref[...] + 0.5 * acc_ref[...]
        if final:
            y = _rms(y, fg_ref[...])
        o_ref[...] = y


def _ffn(x, g, wg, wu, wd, layer, final_g=None):
    t, d = x.shape
    ff = wg.shape[-1]
    tm = min(512, t)
    tf = 512
    final = final_g is not None
    in_specs = [
        pl.BlockSpec((tm, d), lambda i, j: (i, 0)),
        pl.BlockSpec((1, d), lambda i, j: (0, 0)),
        pl.BlockSpec((d, tf), lambda i, j: (0, j)),
        pl.BlockSpec((d, tf), lambda i, j: (0, j)),
        pl.BlockSpec((tf, d), lambda i, j: (j, 0)),
    ]
    args = [x, g.reshape(1, d), _cast_bf16(wg, layer), _cast_bf16(wu, layer), _cast_bf16(wd, layer)]
    if final:
        in_specs.append(pl.BlockSpec((1, d), lambda i, j: (0, 0)))
        args.append(final_g.reshape(1, d))
    return pl.pallas_call(
        functools.partial(_ffn_kernel, final=final),
        grid=(t // tm, ff // tf),
        in_specs=in_specs,
        out_specs=pl.BlockSpec((tm, d), lambda i, j: (i, 0)),
        out_shape=jax.ShapeDtypeStruct((t, d), F32),
        scratch_shapes=[pltpu.VMEM((tm, d), BF16), pltpu.VMEM((tm, d), F32)],
        compiler_params=_params(("parallel", "arbitrary")),
        name="ffn",
    )(*args)


def _norm_mm_kernel(x_ref, g_ref, w_ref, o_ref, xn_ref):
    @pl.when(pl.program_id(1) == 0)
    def _():
        xn_ref[...] = _rms(x_ref[...], g_ref[...]).astype(BF16)

    o_ref[...] = _dot(xn_ref[...], w_ref[...])


def _norm_matmul(x, g, w, tn):
    t, d = x.shape
    n = w.shape[1]
    tm = min(1024, t)
    return pl.pallas_call(
        _norm_mm_kernel,
        grid=(t // tm, n // tn),
        in_specs=[
            pl.BlockSpec((tm, d), lambda i, j: (i, 0)),
            pl.BlockSpec((1, d), lambda i, j: (0, 0)),
            pl.BlockSpec((d, tn), lambda i, j: (0, j)),
        ],
        out_specs=pl.BlockSpec((tm, tn), lambda i, j: (i, j)),
        out_shape=jax.ShapeDtypeStruct((t, n), F32),
        scratch_shapes=[pltpu.VMEM((tm, d), BF16)],
        compiler_params=_params(("parallel", "arbitrary")),
        name="norm_matmul",
    )(x, g.reshape(1, d), w)


def _conv_kernel(pv, pg, hvp, hgp, hvn, hgn, w_ref, o_ref, scr, *, ts):
    i = pl.program_id(1)
    last = pl.num_programs(1) - 1
    prev = jnp.where(i > 0, hvp[0] * _sigmoid(hgp[0]), 0.0)
    nxt = jnp.where(i < last, hvn[0] * _sigmoid(hgn[0]), 0.0)
    scr[0:CONV_HALO, :] = prev
    scr[CONV_HALO:CONV_HALO + ts, :] = pv[0] * _sigmoid(pg[0])
    scr[CONV_HALO + ts:2 * CONV_HALO + ts, :] = nxt
    acc = w_ref[0:1, :] * scr[1:1 + ts, :]
    for k in range(1, CONV_WIDTH):
        acc = acc + w_ref[k:k + 1, :] * scr[1 + k:1 + k + ts, :]
    o_ref[0] = acc


def _glu_conv(p3, conv_w, ts):
    b, s, _ = p3.shape
    nc = CONV_CH // LANES
    r = ts // CONV_HALO
    nh = s // CONV_HALO

    def main(off):
        return pl.BlockSpec((1, ts, LANES), lambda bb, i, c: (bb, i, off + c))

    def prev(off):
        return pl.BlockSpec((1, CONV_HALO, LANES),
                            lambda bb, i, c: (bb, jnp.maximum(i * r - 1, 0), off + c))

    def nxt(off):
        return pl.BlockSpec((1, CONV_HALO, LANES),
                            lambda bb, i, c: (bb, jnp.minimum((i + 1) * r, nh - 1), off + c))

    return pl.pallas_call(
        functools.partial(_conv_kernel, ts=ts),
        grid=(b, s // ts, nc),
        in_specs=[main(0), main(nc), prev(0), prev(nc), nxt(0), nxt(nc),
                  pl.BlockSpec((CONV_WIDTH, LANES), lambda bb, i, c: (0, c))],
        out_specs=pl.BlockSpec((1, ts, LANES), lambda bb, i, c: (bb, i, c)),
        out_shape=jax.ShapeDtypeStruct((b, s, CONV_CH), F32),
        scratch_shapes=[pltpu.VMEM((ts + 2 * CONV_HALO, LANES), F32)],
        compiler_params=_params(("parallel", "parallel", "parallel")),
        name="glu_conv",
    )(p3, p3, p3, p3, p3, p3, conv_w)


def _short_conv_kernel(x_ref, hp, hn, w_ref, o_ref, scr, *, ts):
    i = pl.program_id(1)
    c = pl.program_id(2)
    last = pl.num_programs(1) - 1
    scr[0:SHORT_HALO, :] = jnp.where(i > 0, hp[0], 0.0)
    scr[SHORT_HALO:SHORT_HALO + ts, :] = x_ref[0]
    scr[SHORT_HALO + ts:2 * SHORT_HALO + ts, :] = jnp.where(i < last, hn[0], 0.0)
    y = (w_ref[0:1, :] * scr[SHORT_HALO - 1:SHORT_HALO - 1 + ts, :]
         + w_ref[1:2, :] * scr[SHORT_HALO:SHORT_HALO + ts, :]
         + w_ref[2:3, :] * scr[SHORT_HALO + 1:SHORT_HALO + 1 + ts, :])
    y = _silu(y)

    @pl.when(c == 2)
    def _():
        o_ref[0] = y

    @pl.when(c < 2)
    def _():
        scale = jnp.where(c == 0, DN_DK ** -0.5, 1.0)
        for h in range(DN_HEADS):
            sl = slice(h * DN_DK, (h + 1) * DN_DK)
            yh = y[:, sl]
            o_ref[0, :, sl] = yh * (lax.rsqrt(jnp.sum(yh * yh, axis=-1, keepdims=True) + EPS) * scale)


def _short_conv(p3, short_w, ts):
    b, s, _ = p3.shape
    w = DN_HEADS * DN_DK
    nc = short_w.shape[1] // w
    off = 2 * CONV_CH // w
    r = ts // SHORT_HALO
    nh = s // SHORT_HALO
    return pl.pallas_call(
        functools.partial(_short_conv_kernel, ts=ts),
        grid=(b, s // ts, nc),
        in_specs=[
            pl.BlockSpec((1, ts, w), lambda bb, i, c: (bb, i, off + c)),
            pl.BlockSpec((1, SHORT_HALO, w),
                         lambda bb, i, c: (bb, jnp.maximum(i * r - 1, 0), off + c)),
            pl.BlockSpec((1, SHORT_HALO, w),
                         lambda bb, i, c: (bb, jnp.minimum((i + 1) * r, nh - 1), off + c)),
            pl.BlockSpec((SHORT_CONV, w), lambda bb, i, c: (0, c)),
        ],
        out_specs=pl.BlockSpec((1, ts, w), lambda bb, i, c: (bb, i, c)),
        out_shape=jax.ShapeDtypeStruct((b, s, nc * w), F32),
        scratch_shapes=[pltpu.VMEM((ts + 2 * SHORT_HALO, w), F32)],
        compiler_params=_params(("parallel", "parallel", "parallel")),
        name="short_conv",
    )(p3, p3, p3, short_w)


def _delta_prep_kernel(q_ref, k_ref, v_ref, sc_ref, alog_ref, dtb_ref, pr_ref, qo_ref, gl_ref, *, cpb):
    for ci in range(cpb):
        _delta_prep_chunk(ci, q_ref, k_ref, v_ref, sc_ref, alog_ref, dtb_ref, pr_ref, qo_ref, gl_ref)


def _delta_prep_chunk(ci, q_ref, k_ref, v_ref, sc_ref, alog_ref, dtb_ref, pr_ref, qo_ref, gl_ref):
    rows = slice(ci * CHUNK, (ci + 1) * CHUNK)
    sc = sc_ref[0, rows, :]
    beta = _sigmoid(sc)
    araw = pltpu.roll(sc, LANES - 2 * DN_HEADS, 1)
    g = -jnp.exp(alog_ref[...]) * _softplus(araw + dtb_ref[...])

    ii = lax.broadcasted_iota(jnp.int32, (CHUNK, CHUNK), 0)
    jj = lax.broadcasted_iota(jnp.int32, (CHUNK, CHUNK), 1)
    tri = jnp.concatenate([(ii >= jj).astype(BF16), (ii <= jj).astype(BF16)], axis=0)
    gsum = _dot(tri, jnp.concatenate(_split3(g), axis=1))
    gsum = gsum[:, 0:LANES] + gsum[:, LANES:2 * LANES] + gsum[:, 2 * LANES:]
    gcf = gsum[0:CHUNK]
    gcb = gsum[CHUNK:]
    lane = lax.broadcasted_iota(jnp.int32, (CHUNK, LANES), 1)
    gc = jnp.where(lane < DN_HEADS, gcf, gcb)
    gtot = jnp.broadcast_to(gcf[CHUNK - 1:CHUNK, :], (CHUNK, LANES))
    gexp = jnp.exp(gc)
    kdec = jnp.exp(gtot - gc)
    gl = jnp.exp(gtot)

    nch = 2 * DN_HEADS
    wcat = nch * CHUNK
    er = lax.broadcasted_iota(jnp.int32, (LANES, wcat), 0)
    el = lax.broadcasted_iota(jnp.int32, (LANES, wcat), 1)
    expand = (jnp.right_shift(el, 6) == er).astype(BF16)
    bhi, bmid, _ = _split3(beta)
    spread = _dot(jnp.concatenate(_split3(gc) + (bhi, bmid), axis=0), expand)
    gcat = spread[0:CHUNK] + spread[CHUNK:2 * CHUNK] + spread[2 * CHUNK:3 * CHUNK]
    bcat = spread[3 * CHUNK:4 * CHUNK] + spread[4 * CHUNK:]
    tt = lax.broadcasted_iota(jnp.int32, (CHUNK, wcat), 0)
    ll = lax.broadcasted_iota(jnp.int32, (CHUNK, wcat), 1)
    jl = jnp.bitwise_and(ll, CHUNK - 1)
    diag = tt == jl
    rowcat = jnp.broadcast_to(jnp.sum(jnp.where(diag, gcat, 0.0), axis=0, keepdims=True),
                              (CHUNK, wcat))
    ahead = jnp.where(ll < DN_HEADS * CHUNK, tt - jl, jl - tt)
    incl = ahead >= 0
    strict = ahead > 0
    dec = jnp.exp(jnp.where(incl, gcat - rowcat, -jnp.inf))

    kks, qks, ks, qs, vs = [], [], [], [], []
    for h in range(DN_HEADS):
        sl = slice(h * DN_DK, (h + 1) * DN_DK)
        qh = q_ref[0, rows, sl]
        kh = k_ref[0, rows, sl]
        khb = kh.astype(BF16)
        gram = _dot_nt(jnp.concatenate([khb, qh.astype(BF16)], axis=0), khb)
        kks.append(gram[0:CHUNK])
        qks.append(gram[CHUNK:])
        ks.append(kh)
        qs.append(qh)
        vs.append(v_ref[0, rows, sl])
    kkcat = jnp.concatenate(kks + kks, axis=1)
    qkcat = jnp.concatenate(qks + qks, axis=1)
    mcat = jnp.where(strict, -(kkcat * bcat) * dec, 0.0)
    acat = jnp.where(incl, qkcat * dec, 0.0)

    gw = 4 * CHUNK
    br = lax.broadcasted_iota(jnp.int32, (gw, gw), 0)
    bc = lax.broadcasted_iota(jnp.int32, (gw, gw), 1)
    bmask = jnp.right_shift(br, 6) == jnp.right_shift(bc, 6)

    def blockdiag(xg):
        xb = xg.astype(BF16)
        return jnp.where(bmask, jnp.concatenate([xb] * 4, axis=0), jnp.zeros((), BF16))

    ngrp = wcat // gw
    xs = [mcat[:, g * gw:(g + 1) * gw] for g in range(ngrp)]
    tinv = [jnp.where(diag[:, g * gw:(g + 1) * gw], 1.0, 0.0) + xs[g] for g in range(ngrp)]
    xbd = [blockdiag(x) for x in xs]
    for _ in range(5):
        xs = [_dot(xs[g].astype(BF16), xbd[g]) for g in range(ngrp)]
        xbd = [blockdiag(x) for x in xs]
        tinv = [tinv[g] + _dot(tinv[g].astype(BF16), xbd[g]) for g in range(ngrp)]

    rhs, kds, qds = [], [], []
    for c in range(nch):
        h = c % DN_HEADS
        bcol = beta[:, c:c + 1]
        gex = gexp[:, c:c + 1]
        rhs.append(jnp.concatenate([vs[h] * bcol, ks[h] * (bcol * gex)], axis=1).astype(BF16))
        kds.append((ks[h] * kdec[:, c:c + 1]).astype(BF16))
        qds.append(qs[h] * gex)
    uws = []
    for c in range(nch):
        g, o = divmod(c, 4)
        tc = tinv[g][:, o * CHUNK:(o + 1) * CHUNK]
        uws.append(_dot(tc.astype(BF16), rhs[c]).astype(BF16))
    auw = [_dot(acat[:, c * CHUNK:(c + 1) * CHUNK].astype(BF16), uws[c]) for c in range(nch)]
    kuw = [lax.dot_general(kds[c], uws[c], (((0,), (0,)), ((), ())), preferred_element_type=F32)
           for c in range(nch)]
    for c in range(nch):
        pr_ref[0, ci, c] = jnp.concatenate([qds[c] - auw[c][:, DN_DV:], -kuw[c][:, DN_DV:]],
                                           axis=0).astype(BF16)
        qo_ref[0, ci, c] = jnp.concatenate([auw[c][:, 0:DN_DV], kuw[c][:, 0:DN_DV]], axis=0).astype(BF16)
        gl_ref[0, ci, c] = jnp.broadcast_to(gl[0:8, c:c + 1], (8, LANES))


def _delta_prep(qkvn, p3, a_log, dt_bias):
    b, s, _ = qkvn.shape
    n = s // CHUNK
    nch = 2 * DN_HEADS
    w = DN_HEADS * DN_DK

    def pad_row(a):
        return jnp.pad(a.reshape(1, nch), ((0, 0), (0, LANES - nch)))

    cpb = 2
    rows = cpb * CHUNK
    return pl.pallas_call(
        functools.partial(_delta_prep_kernel, cpb=cpb),
        grid=(b, n // cpb),
        in_specs=[
            pl.BlockSpec((1, rows, w), lambda bb, i: (bb, i, 0)),
            pl.BlockSpec((1, rows, w), lambda bb, i: (bb, i, 1)),
            pl.BlockSpec((1, rows, w), lambda bb, i: (bb, i, 2)),
            pl.BlockSpec((1, rows, LANES), lambda bb, i: (bb, i, SCAL_COL_BLOCK)),
            pl.BlockSpec((1, LANES), lambda bb, i: (0, 0)),
            pl.BlockSpec((1, LANES), lambda bb, i: (0, 0)),
        ],
        out_specs=[
            pl.BlockSpec((1, cpb, nch, CHUNK + DN_DK, DN_DV), lambda bb, i: (bb, i, 0, 0, 0)),
            pl.BlockSpec((1, cpb, nch, CHUNK + DN_DK, DN_DV), lambda bb, i: (bb, i, 0, 0, 0)),
            pl.BlockSpec((1, cpb, nch, 8, LANES), lambda bb, i: (bb, i, 0, 0, 0)),
        ],
        out_shape=[
            jax.ShapeDtypeStruct((b, n, nch, CHUNK + DN_DK, DN_DV), BF16),
            jax.ShapeDtypeStruct((b, n, nch, CHUNK + DN_DK, DN_DV), BF16),
            jax.ShapeDtypeStruct((b, n, nch, 8, LANES), F32),
        ],
        compiler_params=_params(("parallel", "parallel")),
        name="delta_prep",
    )(qkvn, qkvn, qkvn, p3, pad_row(a_log), pad_row(dt_bias))


def _delta_scan_kernel(prf, qof, glf, prb, qob, glb, of_ref, ob_ref, st_ref, *, cb):
    @pl.when(pl.program_id(1) == 0)
    def _():
        st_ref[...] = jnp.zeros_like(st_ref)

    for step in range(cb):
        for d, (pr_r, qo_r, gl_r, o_r) in enumerate(((prf, qof, glf, of_ref), (prb, qob, glb, ob_ref))):
            ci = step if d == 0 else cb - 1 - step
            for h in range(DN_HEADS):
                ch = d * DN_HEADS + h
                st = st_ref[ch]
                r = _dot(pr_r[0, ci, h], st.astype(BF16)) + qo_r[0, ci, h]
                o_r[0, ci * CHUNK:(ci + 1) * CHUNK, h * DN_DV:(h + 1) * DN_DV] = r[0:CHUNK]
                st_ref[ch] = st * gl_r[0, ci, h, 0:1, :] + r[CHUNK:]


def _delta_scan(pr, qo, gl, cb=2):
    b, n = pr.shape[0], pr.shape[1]
    s = n * CHUNK
    nb = n // cb
    hb = DN_HEADS

    def spec(shape, back):
        if back:
            return pl.BlockSpec(shape, lambda bb, i: (bb, nb - 1 - i, 1) + (0,) * (len(shape) - 3))
        return pl.BlockSpec(shape, lambda bb, i: (bb, i, 0) + (0,) * (len(shape) - 3))

    shapes = [(1, cb, hb, CHUNK + DN_DK, DN_DV), (1, cb, hb, CHUNK + DN_DK, DN_DV), (1, cb, hb, 8, LANES)]
    in_specs = [spec(sh, False) for sh in shapes] + [spec(sh, True) for sh in shapes]
    ow = hb * DN_DV
    return pl.pallas_call(
        functools.partial(_delta_scan_kernel, cb=cb),
        grid=(b, nb),
        in_specs=in_specs,
        out_specs=[pl.BlockSpec((1, cb * CHUNK, ow), lambda bb, i: (bb, i, 0)),
                   pl.BlockSpec((1, cb * CHUNK, ow), lambda bb, i: (bb, nb - 1 - i, 0))],
        out_shape=[jax.ShapeDtypeStruct((b, s, ow), F32)] * 2,
        scratch_shapes=[pltpu.VMEM((2 * hb, DN_DK, DN_DV), F32)],
        compiler_params=_params(("parallel", "arbitrary")),
        name="delta_scan",
    )(pr, qo, gl, pr, qo, gl)


def _even_out_kernel(cv_ref, cb_ref, lg_ref, lb_ref, of_ref, ob_ref, z_ref, og_ref, w_ref, x_ref, o_ref):
    a = cv_ref[...] + cb_ref[...]
    mu = jnp.mean(a, axis=-1, keepdims=True)
    xc = a - mu
    y = xc * lax.rsqrt(jnp.mean(xc * xc, axis=-1, keepdims=True) + EPS)
    acc = x_ref[...] + _dot(_silu(y * lg_ref[...] + lb_ref[...]).astype(BF16), w_ref[0:CONV_CH, :])
    gated = []
    for h in range(DN_HEADS):
        sl = slice(h * DN_DV, (h + 1) * DN_DV)
        o = of_ref[:, sl] + ob_ref[:, sl]
        gated.append((_rms(o, og_ref[...]) * _silu(z_ref[:, sl])).astype(BF16))
    o_ref[...] = acc + _dot(jnp.concatenate(gated, axis=1), w_ref[CONV_CH:, :])


def _even_out(conv, conv_b, ln_g, ln_b, o_f, o_b, p, onorm_g, w_out, x):
    t, d = x.shape
    tm = min(256, t)
    kk = CONV_CH + DN_HEADS * DN_DV
    zblk = (2 * CONV_CH + 3 * DN_HEADS * DN_DK) // CONV_CH

    def row(n):
        return pl.BlockSpec((1, n), lambda i: (0, 0))

    def tile(n, jblk=0):
        return pl.BlockSpec((tm, n), lambda i: (i, jblk))

    return pl.pallas_call(
        _even_out_kernel,
        grid=(t // tm,),
        in_specs=[tile(CONV_CH), row(CONV_CH), row(CONV_CH), row(CONV_CH),
                  tile(CONV_CH), tile(CONV_CH), tile(CONV_CH, zblk), row(DN_DV),
                  pl.BlockSpec((kk, d), lambda i: (0, 0)),
                  tile(d)],
        out_specs=tile(d),
        out_shape=jax.ShapeDtypeStruct((t, d), F32),
        compiler_params=_params(("parallel",)),
        name="even_out",
    )(conv, conv_b.reshape(1, -1), ln_g.reshape(1, -1), ln_b.reshape(1, -1), o_f, o_b, p,
      onorm_g.reshape(1, -1), w_out, x)


def _even_mixer(x, b, s, norm_g, w_in, conv_w, conv_b, ln_g, ln_b, short_w, a_log, dt_bias,
                onorm_g, w_out):
    t = b * s
    p = _norm_matmul(x, norm_g, w_in, tn=5 * MXU_WIDTH)
    p3 = p.reshape(b, s, EVEN_IN_PAD)
    ts = min(512, s)
    conv = _glu_conv(p3, conv_w, ts)
    qkvn = _short_conv(p3, short_w, ts)
    pr, qo, gl = _delta_prep(qkvn, p3, a_log, dt_bias)
    o_f, o_b = _delta_scan(pr, qo, gl)
    return _even_out(conv.reshape(t, -1), conv_b, ln_g, ln_b, o_f.reshape(t, -1), o_b.reshape(t, -1),
                     p, onorm_g, w_out, x)


def _qkv_rope_kernel(xin_ref, g_ref, w_ref, pos_ref, freq_ref, o_ref, xn_ref, cs_ref):
    j = pl.program_id(1)
    nrep = o_ref.shape[1] // LANES

    @pl.when(j == 0)
    def _():
        xn_ref[...] = _rms(xin_ref[...], g_ref[...]).astype(BF16)
        ang = pos_ref[...].astype(F32) * freq_ref[...]
        lane = lax.broadcasted_iota(jnp.int32, ang.shape, 1)
        half = ROPE_DIMS // 2
        cs_ref[0] = jnp.where(lane < ROPE_DIMS, jnp.cos(ang), 1.0)
        sn = jnp.sin(ang)
        cs_ref[1] = jnp.where(lane < half, -sn, 0.0)
        cs_ref[2] = jnp.where((lane >= half) & (lane < ROPE_DIMS), sn, 0.0)

    nq = DA_HEADS * 2 * DA_DH // o_ref.shape[1]

    rot = j < 2 * nq
    half = ROPE_DIMS // 2
    w = o_ref.shape[1] // 2
    qs = jnp.where(j < nq, DA_DH ** -0.5 * math.log2(math.e), 1.0)
    c = jnp.concatenate([jnp.where(rot, cs_ref[0], 1.0) * qs] * (nrep // 2), axis=1)
    s1 = jnp.concatenate([jnp.where(rot, cs_ref[1], 0.0) * qs] * (nrep // 2), axis=1)
    s2 = jnp.concatenate([jnp.where(rot, cs_ref[2], 0.0) * qs] * (nrep // 2), axis=1)
    xn = xn_ref[...]
    for part in range(2):
        x = _dot(xn, w_ref[:, part * w:(part + 1) * w])
        o_ref[:, part * w:(part + 1) * w] = (
            x * c + pltpu.roll(x, w - half, 1) * s1 + pltpu.roll(x, half, 1) * s2).astype(BF16)


def _qkv_rope(x, g, w, pos, freq):
    t, d = x.shape
    n = w.shape[1]
    tm = min(1024, t)
    tn = 4 * MXU_WIDTH
    return pl.pallas_call(
        _qkv_rope_kernel,
        grid=(t // tm, n // tn),
        in_specs=[pl.BlockSpec((tm, d), lambda i, j: (i, 0)),
                  pl.BlockSpec((1, d), lambda i, j: (0, 0)),
                  pl.BlockSpec((d, tn), lambda i, j: (0, j)),
                  pl.BlockSpec((tm, 1), lambda i, j: (i, 0)),
                  pl.BlockSpec((1, LANES), lambda i, j: (0, 0))],
        out_specs=pl.BlockSpec((tm, tn), lambda i, j: (i, j)),
        out_shape=jax.ShapeDtypeStruct((t, n), BF16),
        scratch_shapes=[pltpu.VMEM((tm, d), BF16), pltpu.VMEM((3, tm, LANES), F32)],
        compiler_params=_params(("parallel", "arbitrary")),
        name="qkv_rope",
    )(x, g.reshape(1, d), w, pos, freq)


def _attn_kernel(q_ref, k_ref, v_ref, lq1, lk1, lq2, lk2, sg_ref, o_ref, *, lambda_init, nsplit):
    lam = (jnp.exp(jnp.sum(lq1[...] * lk1[...], axis=-1, keepdims=True))
           - jnp.exp(jnp.sum(lq2[...] * lk2[...], axis=-1, keepdims=True)) + lambda_init)
    k = k_ref[...]
    v = v_ref[...]
    rows = q_ref.shape[0] // nsplit
    for part in range(nsplit):
        rs = slice(part * rows, (part + 1) * rows)
        q = q_ref[rs, :]
        s1 = _dot_nt(q[:, 0:DA_DH], k[:, 0:DA_DH])
        s2 = _dot_nt(q[:, DA_DH:], k[:, DA_DH:])
        e1 = jnp.exp2(s1 - jnp.max(s1, axis=-1, keepdims=True))
        e2 = jnp.exp2(s2 - jnp.max(s2, axis=-1, keepdims=True))
        r1 = 1.0 / jnp.sum(e1, axis=-1, keepdims=True)
        r2 = lam / jnp.sum(e2, axis=-1, keepdims=True)
        o = _dot(e1.astype(BF16), v) * r1 - _dot(e2.astype(BF16), v) * r2
        o_ref[rs, :] = (_rms(o, sg_ref[...]) * (1.0 - lambda_init)).astype(BF16)


def _diff_attention(qkvb, b, s, lq1, lk1, lq2, lk2, subln_g, lambda_init):
    t = b * s
    hw = 2 * DA_DH
    tq = min(1024, s)
    nq = s // tq

    def row(n):
        return pl.BlockSpec((1, n), lambda bb, h, i: (0, 0))

    return pl.pallas_call(
        functools.partial(_attn_kernel, lambda_init=lambda_init, nsplit=max(1, tq // 128)),
        grid=(b, DA_HEADS, nq),
        in_specs=[pl.BlockSpec((tq, hw), lambda bb, h, i: (bb * nq + i, h)),
                  pl.BlockSpec((s, hw), lambda bb, h, i: (bb, DA_HEADS + h)),
                  pl.BlockSpec((s, hw), lambda bb, h, i: (bb, 2 * DA_HEADS + h)),
                  row(DA_DH), row(DA_DH), row(DA_DH), row(DA_DH), row(hw)],
        out_specs=pl.BlockSpec((tq, hw), lambda bb, h, i: (bb * nq + i, h)),
        out_shape=jax.ShapeDtypeStruct((t, DA_HEADS * hw), BF16),
        compiler_params=_params(("parallel", "parallel", "arbitrary")),
        name="diff_attention",
    )(qkvb, qkvb, qkvb, lq1.reshape(1, -1), lk1.reshape(1, -1), lq2.reshape(1, -1),
      lk2.reshape(1, -1), subln_g.reshape(1, -1))


def _mm_res_kernel(a_ref, w_ref, x_ref, o_ref):
    o_ref[...] = x_ref[...] + _dot(a_ref[...], w_ref[...])


def _matmul_residual(a, w, x):
    t, d = x.shape
    kk = a.shape[1]
    tm = min(512, t)
    tn = 1024
    return pl.pallas_call(
        _mm_res_kernel,
        grid=(t // tm, d // tn),
        in_specs=[pl.BlockSpec((tm, kk), lambda i, j: (i, 0)),
                  pl.BlockSpec((kk, tn), lambda i, j: (0, j)),
                  pl.BlockSpec((tm, tn), lambda i, j: (i, j))],
        out_specs=pl.BlockSpec((tm, tn), lambda i, j: (i, j)),
        out_shape=jax.ShapeDtypeStruct((t, d), F32),
        compiler_params=_params(("parallel", "arbitrary")),
        name="matmul_residual",
    )(a, w, x)


def _odd_mixer(x, b, s, positions, norm_g, w_qkv, lq1, lk1, lq2, lk2, subln_g, w_o, lambda_init):
    t = b * s
    inv_freq = ROPE_THETA ** (-jnp.arange(0, ROPE_DIMS, 2, dtype=F32) / ROPE_DIMS)
    freq = jnp.concatenate([inv_freq, inv_freq, jnp.zeros((LANES - ROPE_DIMS,), F32)]).reshape(1, LANES)
    qkvb = _qkv_rope(x, norm_g, w_qkv, positions.reshape(t, 1), freq)
    attn = _diff_attention(qkvb, b, s, lq1, lk1, lq2, lk2, subln_g, lambda_init)
    return _matmul_residual(attn, w_o, x)


def kernel(x, positions, norm_ffn1, ffn1_wg, ffn1_wu, ffn1_wd, norm_mix, norm_ffn2, ffn2_wg, ffn2_wu, ffn2_wd, ev_w_in, ev_conv_w, ev_conv_b, ev_ln_g, ev_ln_b, ev_short_w, ev_a_log, ev_dt_bias, ev_onorm_g, ev_w_out, od_w_qkv, od_lq1, od_lk1, od_lq2, od_lk2, od_subln_g, od_w_o, final_norm):
    b, s, d = x.shape
    depth = norm_ffn1.shape[0]
    xf = x.reshape(b * s, d)
    for i in range(depth):
        xf = _ffn(xf, norm_ffn1[i], ffn1_wg, ffn1_wu, ffn1_wd, i)
        j = i // 2
        if i % 2 == 0:
            xf = _even_mixer(xf, b, s, norm_mix[i], _cast_bf16(ev_w_in, j, EVEN_IN_PAD), ev_conv_w[j],
                             ev_conv_b[j], ev_ln_g[j],
                             ev_ln_b[j], ev_short_w[j], ev_a_log[j], ev_dt_bias[j], ev_onorm_g[j],
                             _cast_bf16(ev_w_out, j))
        else:
            lambda_init = 0.8 - 0.6 * math.exp(-0.3 * i)
            xf = _odd_mixer(xf, b, s, positions, norm_mix[i], _cast_bf16(od_w_qkv, j), od_lq1[j], od_lk1[j],
                            od_lq2[j], od_lk2[j], od_subln_g[j], _cast_bf16(od_w_o, j), lambda_init)
        xf = _ffn(xf, norm_ffn2[i], ffn2_wg, ffn2_wu, ffn2_wd, i,
                  final_g=final_norm if i == depth - 1 else None)
    return xf.reshape(b, s, d)
```

```python
import functools
import math

import jax
import jax.numpy as jnp
from jax import lax
from jax.experimental import pallas as pl
from jax.experimental.pallas import tpu as pltpu

F32 = jnp.float32
BF16 = jnp.bfloat16
EPS = 1e-6

D_FF = 5632
CONV_CH = 1024
CONV_WIDTH = 31
DN_HEADS = 8
DN_DK = 128
DN_DV = 128
SHORT_CONV = 3
CHUNK = 64
DA_HEADS = 8
DA_DH = 128
ROPE_THETA = 500000.0
ROPE_DIMS = DA_DH // 4

LANES = 128
CONV_HALO = 16
SHORT_HALO = 8
MXU_WIDTH = 256
EVEN_IN_PAD = 6400
SCAL_COL_BLOCK = 6144 // LANES
VMEM_LIMIT = 56 * 1024 * 1024
HIGHEST = lax.Precision.HIGHEST


def _sigmoid(x):
    return 1.0 / (1.0 + jnp.exp(-x))


def _silu(x):
    return x * _sigmoid(x)


def _softplus(x):
    return jnp.maximum(x, 0.0) + jnp.log(1.0 + jnp.exp(-jnp.abs(x)))


def _rms(x, g):
    ms = jnp.mean(x * x, axis=-1, keepdims=True)
    return x * lax.rsqrt(ms + EPS) * g


def _dot(a, b):
    return jnp.dot(a, b, preferred_element_type=F32)


def _dot_nt(a, b, precision=None):
    return lax.dot_general(a, b, (((1,), (1,)), ((), ())), precision=precision,
                           preferred_element_type=F32)


def _split3(x):
    hi = x.astype(BF16)
    r1 = x - hi.astype(F32)
    mid = r1.astype(BF16)
    lo = (r1 - mid.astype(F32)).astype(BF16)
    return hi, mid, lo


def _params(sem):
    return pltpu.CompilerParams(dimension_semantics=sem, vmem_limit_bytes=VMEM_LIMIT)


def _cast_kernel(w_ref, o_ref):
    cols = w_ref.shape[1]
    o_ref[:, 0:cols] = w_ref[...].astype(BF16)
    if o_ref.shape[1] > cols:
        o_ref[:, cols:] = jnp.zeros((o_ref.shape[0], o_ref.shape[1] - cols), BF16)


def _cast_bf16(w, layer, pad_cols=None):
    rows, cols = w.shape[-2:]
    ocols = cols if pad_cols is None else pad_cols
    nblk = 8
    tr = rows // nblk
    return pl.pallas_call(
        _cast_kernel,
        grid=(nblk,),
        in_specs=[pl.BlockSpec((None, tr, cols), lambda r: (layer, r, 0))],
        out_specs=pl.BlockSpec((tr, ocols), lambda r: (r, 0)),
        out_shape=jax.ShapeDtypeStruct((rows, ocols), BF16),
        compiler_params=_params(("parallel",)),
        name="cast_bf16",
    )(w)


def _ffn_kernel(x_ref, g_ref, wg_ref, wu_ref, wd_ref, *rest, final):
    if final:
        fg_ref, o_ref, xn_ref, acc_ref = rest
    else:
        o_ref, xn_ref, acc_ref = rest
    j = pl.program_id(1)

    @pl.when(j == 0)
    def _():
        xn_ref[...] = _rms(x_ref[...], g_ref[...]).astype(BF16)
        acc_ref[...] = jnp.zeros_like(acc_ref)

    xn = xn_ref[...]
    a = _dot(xn, wg_ref[...])
    b = _dot(xn, wu_ref[...])
    h = (_silu(a) * b).astype(BF16)
    acc_ref[...] += _dot(h, wd_ref[...])

    @pl.when(j == pl.num_programs(1) - 1)
    def _():
        y = x_ref[...] + 0.5 * acc_ref[...]
        if final:
            y = _rms(y, fg_ref[...])
        o_ref[...] = y


def _ffn(x, g, wg, wu, wd, layer, final_g=None):
    t, d = x.shape
    ff = wg.shape[-1]
    tm = min(512, t)
    tf = 512
    final = final_g is not None
    in_specs = [
        pl.BlockSpec((tm, d), lambda i, j: (i, 0)),
        pl.BlockSpec((1, d), lambda i, j: (0, 0)),
        pl.BlockSpec((d, tf), lambda i, j: (0, j)),
        pl.BlockSpec((d, tf), lambda i, j: (0, j)),
        pl.BlockSpec((tf, d), lambda i, j: (j, 0)),
    ]
    args = [x, g.reshape(1, d), _cast_bf16(wg, layer), _cast_bf16(wu, layer), _cast_bf16(wd, layer)]
    if final:
        in_specs.append(pl.BlockSpec((1, d), lambda i, j: (0, 0)))
        args.append(final_g.reshape(1, d))
    return pl.pallas_call(
        functools.partial(_ffn_kernel, final=final),
        grid=(t // tm, ff // tf),
        in_specs=in_specs,
        out_specs=pl.BlockSpec((tm, d), lambda i, j: (i, 0)),
        out_shape=jax.ShapeDtypeStruct((t, d), F32),
        scratch_shapes=[pltpu.VMEM((tm, d), BF16), pltpu.VMEM((tm, d), F32)],
        compiler_params=_params(("parallel", "arbitrary")),
        name="ffn",
    )(*args)


def _norm_mm_kernel(x_ref, g_ref, w_ref, o_ref, xn_ref):
    @pl.when(pl.program_id(1) == 0)
    def _():
        xn_ref[...] = _rms(x_ref[...], g_ref[...]).astype(BF16)

    o_ref[...] = _dot(xn_ref[...], w_ref[...])


def _norm_matmul(x, g, w, tn):
    t, d = x.shape
    n = w.shape[1]
    tm = min(1024, t)
    return pl.pallas_call(
        _norm_mm_kernel,
        grid=(t // tm, n // tn),
        in_specs=[
            pl.BlockSpec((tm, d), lambda i, j: (i, 0)),
            pl.BlockSpec((1, d), lambda i, j: (0, 0)),
            pl.BlockSpec((d, tn), lambda i, j: (0, j)),
        ],
        out_specs=pl.BlockSpec((tm, tn), lambda i, j: (i, j)),
        out_shape=jax.ShapeDtypeStruct((t, n), F32),
        scratch_shapes=[pltpu.VMEM((tm, d), BF16)],
        compiler_params=_params(("parallel", "arbitrary")),
        name="norm_matmul",
    )(x, g.reshape(1, d), w)


def _conv_kernel(pv, pg, hvp, hgp, hvn, hgn, w_ref, o_ref, scr, *, ts):
    i = pl.program_id(1)
    last = pl.num_programs(1) - 1
    prev = jnp.where(i > 0, hvp[0] * _sigmoid(hgp[0]), 0.0)
    nxt = jnp.where(i < last, hvn[0] * _sigmoid(hgn[0]), 0.0)
    scr[0:CONV_HALO, :] = prev
    scr[CONV_HALO:CONV_HALO + ts, :] = pv[0] * _sigmoid(pg[0])
    scr[CONV_HALO + ts:2 * CONV_HALO + ts, :] = nxt
    acc = w_ref[0:1, :] * scr[1:1 + ts, :]
    for k in range(1, CONV_WIDTH):
        acc = acc + w_ref[k:k + 1, :] * scr[1 + k:1 + k + ts, :]
    o_ref[0] = acc


def _glu_conv(p3, conv_w, ts):
    b, s, _ = p3.shape
    nc = CONV_CH // LANES
    r = ts // CONV_HALO
    nh = s // CONV_HALO

    def main(off):
        return pl.BlockSpec((1, ts, LANES), lambda bb, i, c: (bb, i, off + c))

    def prev(off):
        return pl.BlockSpec((1, CONV_HALO, LANES),
                            lambda bb, i, c: (bb, jnp.maximum(i * r - 1, 0), off + c))

    def nxt(off):
        return pl.BlockSpec((1, CONV_HALO, LANES),
                            lambda bb, i, c: (bb, jnp.minimum((i + 1) * r, nh - 1), off + c))

    return pl.pallas_call(
        functools.partial(_conv_kernel, ts=ts),
        grid=(b, s // ts, nc),
        in_specs=[main(0), main(nc), prev(0), prev(nc), nxt(0), nxt(nc),
                  pl.BlockSpec((CONV_WIDTH, LANES), lambda bb, i, c: (0, c))],
        out_specs=pl.BlockSpec((1, ts, LANES), lambda bb, i, c: (bb, i, c)),
        out_shape=jax.ShapeDtypeStruct((b, s, CONV_CH), F32),
        scratch_shapes=[pltpu.VMEM((ts + 2 * CONV_HALO, LANES), F32)],
        compiler_params=_params(("parallel", "parallel", "parallel")),
        name="glu_conv",
    )(p3, p3, p3, p3, p3, p3, conv_w)


def _delta_prep_kernel(pq, pk, pv, hqp, hkp, hvp, hqn, hkn, hvn, w_ref, sc_ref, alog_ref, dtb_ref,
                       pr_ref, qo_ref, gl_ref, q_ref, k_ref, v_ref, scr, *, cpb):
    i = pl.program_id(1)
    last = pl.num_programs(1) - 1
    nrow = cpb * CHUNK
    w = DN_HEADS * DN_DK
    for idx, (main, hp, hn, dst) in enumerate(((pq, hqp, hqn, q_ref), (pk, hkp, hkn, k_ref),
                                               (pv, hvp, hvn, v_ref))):
        scr[idx, 0:SHORT_HALO, :] = jnp.where(i > 0, hp[0], 0.0)
        scr[idx, SHORT_HALO:SHORT_HALO + nrow, :] = main[0]
        scr[idx, SHORT_HALO + nrow:2 * SHORT_HALO + nrow, :] = jnp.where(i < last, hn[0], 0.0)
        ws = w_ref[:, idx * w:(idx + 1) * w]
        y = (ws[0:1, :] * scr[idx, SHORT_HALO - 1:SHORT_HALO - 1 + nrow, :]
             + ws[1:2, :] * scr[idx, SHORT_HALO:SHORT_HALO + nrow, :]
             + ws[2:3, :] * scr[idx, SHORT_HALO + 1:SHORT_HALO + 1 + nrow, :])
        y = _silu(y)
        if idx == 2:
            dst[0] = y
        else:
            scale = DN_DK ** -0.5 if idx == 0 else 1.0
            for h in range(DN_HEADS):
                sl = slice(h * DN_DK, (h + 1) * DN_DK)
                yh = y[:, sl]
                dst[0, :, sl] = yh * (lax.rsqrt(jnp.sum(yh * yh, axis=-1, keepdims=True) + EPS) * scale)

    for ci in range(cpb):
        _delta_prep_chunk(ci, q_ref, k_ref, v_ref, sc_ref, alog_ref, dtb_ref, pr_ref, qo_ref, gl_ref)


def _delta_prep_chunk(ci, q_ref, k_ref, v_ref, sc_ref, alog_ref, dtb_ref, pr_ref, qo_ref, gl_ref):
    rows = slice(ci * CHUNK, (ci + 1) * CHUNK)
    sc = sc_ref[0, rows, :]
    beta = _sigmoid(sc)
    araw = pltpu.roll(sc, LANES - 2 * DN_HEADS, 1)
    g = -jnp.exp(alog_ref[...]) * _softplus(araw + dtb_ref[...])

    ii = lax.broadcasted_iota(jnp.int32, (CHUNK, CHUNK), 0)
    jj = lax.broadcasted_iota(jnp.int32, (CHUNK, CHUNK), 1)
    tri = jnp.concatenate([(ii >= jj).astype(BF16), (ii <= jj).astype(BF16)], axis=0)
    gsum = _dot(tri, jnp.concatenate(_split3(g), axis=1))
    gsum = gsum[:, 0:LANES] + gsum[:, LANES:2 * LANES] + gsum[:, 2 * LANES:]
    gcf = gsum[0:CHUNK]
    gcb = gsum[CHUNK:]
    lane = lax.broadcasted_iota(jnp.int32, (CHUNK, LANES), 1)
    gc = jnp.where(lane < DN_HEADS, gcf, gcb)
    gtot = jnp.broadcast_to(gcf[CHUNK - 1:CHUNK, :], (CHUNK, LANES))
    gexp = jnp.exp(gc)
    kdec = jnp.exp(gtot - gc)
    gl = jnp.exp(gtot)

    nch = 2 * DN_HEADS
    wcat = nch * CHUNK
    er = lax.broadcasted_iota(jnp.int32, (LANES, wcat), 0)
    el = lax.broadcasted_iota(jnp.int32, (LANES, wcat), 1)
    expand = (jnp.right_shift(el, 6) == er).astype(BF16)
    bhi, bmid, _ = _split3(beta)
    spread = _dot(jnp.concatenate(_split3(gc) + (bhi, bmid), axis=0), expand)
    gcat = spread[0:CHUNK] + spread[CHUNK:2 * CHUNK] + spread[2 * CHUNK:3 * CHUNK]
    bcat = spread[3 * CHUNK:4 * CHUNK] + spread[4 * CHUNK:]
    tt = lax.broadcasted_iota(jnp.int32, (CHUNK, wcat), 0)
    ll = lax.broadcasted_iota(jnp.int32, (CHUNK, wcat), 1)
    jl = jnp.bitwise_and(ll, CHUNK - 1)
    diag = tt == jl
    rowcat = jnp.broadcast_to(jnp.sum(jnp.where(diag, gcat, 0.0), axis=0, keepdims=True),
                              (CHUNK, wcat))
    ahead = jnp.where(ll < DN_HEADS * CHUNK, tt - jl, jl - tt)
    incl = ahead >= 0
    strict = ahead > 0
    dec = jnp.exp(jnp.where(incl, gcat - rowcat, -jnp.inf))

    kks, qks, ks, qs, vs = [], [], [], [], []
    for h in range(DN_HEADS):
        sl = slice(h * DN_DK, (h + 1) * DN_DK)
        qh = q_ref[0, rows, sl]
        kh = k_ref[0, rows, sl]
        khb = kh.astype(BF16)
        gram = _dot_nt(jnp.concatenate([khb, qh.astype(BF16)], axis=0), khb)
        kks.append(gram[0:CHUNK])
        qks.append(gram[CHUNK:])
        ks.append(kh)
        qs.append(qh)
        vs.append(v_ref[0, rows, sl])
    kkcat = jnp.concatenate(kks + kks, axis=1)
    qkcat = jnp.concatenate(qks + qks, axis=1)
    mcat = jnp.where(strict, -(kkcat * bcat) * dec, 0.0)
    acat = jnp.where(incl, qkcat * dec, 0.0)

    gw = 4 * CHUNK
    br = lax.broadcasted_iota(jnp.int32, (gw, gw), 0)
    bc = lax.broadcasted_iota(jnp.int32, (gw, gw), 1)
    bmask = jnp.right_shift(br, 6) == jnp.right_shift(bc, 6)

    def blockdiag(xg):
        xb = xg.astype(BF16)
        return jnp.where(bmask, jnp.concatenate([xb] * 4, axis=0), jnp.zeros((), BF16))

    ngrp = wcat // gw
    xs = [mcat[:, g * gw:(g + 1) * gw] for g in range(ngrp)]
    tinv = [jnp.where(diag[:, g * gw:(g + 1) * gw], 1.0, 0.0) + xs[g] for g in range(ngrp)]
    xbd = [blockdiag(x) for x in xs]
    for _ in range(5):
        xs = [_dot(xs[g].astype(BF16), xbd[g]) for g in range(ngrp)]
        xbd = [blockdiag(x) for x in xs]
        tinv = [tinv[g] + _dot(tinv[g].astype(BF16), xbd[g]) for g in range(ngrp)]

    rhs, kds, qds = [], [], []
    for c in range(nch):
        h = c % DN_HEADS
        bcol = beta[:, c:c + 1]
        gex = gexp[:, c:c + 1]
        rhs.append(jnp.concatenate([vs[h] * bcol, ks[h] * (bcol * gex)], axis=1).astype(BF16))
        kds.append((ks[h] * kdec[:, c:c + 1]).astype(BF16))
        qds.append(qs[h] * gex)
    uws = []
    for c in range(nch):
        g, o = divmod(c, 4)
        tc = tinv[g][:, o * CHUNK:(o + 1) * CHUNK]
        uws.append(_dot(tc.astype(BF16), rhs[c]).astype(BF16))
    auw = [_dot(acat[:, c * CHUNK:(c + 1) * CHUNK].astype(BF16), uws[c]) for c in range(nch)]
    kuw = [lax.dot_general(kds[c], uws[c], (((0,), (0,)), ((), ())), preferred_element_type=F32)
           for c in range(nch)]
    for c in range(nch):
        pr_ref[0, ci, c] = jnp.concatenate([qds[c] - auw[c][:, DN_DV:], -kuw[c][:, DN_DV:]],
                                           axis=0).astype(BF16)
        qo_ref[0, ci, c] = jnp.concatenate([auw[c][:, 0:DN_DV], kuw[c][:, 0:DN_DV]], axis=0).astype(BF16)
        gl_ref[0, ci, c] = jnp.broadcast_to(gl[0:8, c:c + 1], (8, LANES))


def _delta_prep(p3, short_w, a_log, dt_bias):
    b, s, _ = p3.shape
    n = s // CHUNK
    nch = 2 * DN_HEADS
    w = DN_HEADS * DN_DK
    off = 2 * CONV_CH // w

    def pad_row(a):
        return jnp.pad(a.reshape(1, nch), ((0, 0), (0, LANES - nch)))

    cpb = 2
    rows = cpb * CHUNK
    r = rows // SHORT_HALO
    nh = s // SHORT_HALO

    def main(c):
        return pl.BlockSpec((1, rows, w), lambda bb, i: (bb, i, off + c))

    def prev(c):
        return pl.BlockSpec((1, SHORT_HALO, w), lambda bb, i: (bb, jnp.maximum(i * r - 1, 0), off + c))

    def nxt(c):
        return pl.BlockSpec((1, SHORT_HALO, w), lambda bb, i: (bb, jnp.minimum((i + 1) * r, nh - 1), off + c))

    return pl.pallas_call(
        functools.partial(_delta_prep_kernel, cpb=cpb),
        grid=(b, n // cpb),
        in_specs=[
            main(0), main(1), main(2), prev(0), prev(1), prev(2), nxt(0), nxt(1), nxt(2),
            pl.BlockSpec((SHORT_CONV, 3 * w), lambda bb, i: (0, 0)),
            pl.BlockSpec((1, rows, LANES), lambda bb, i: (bb, i, SCAL_COL_BLOCK)),
            pl.BlockSpec((1, LANES), lambda bb, i: (0, 0)),
            pl.BlockSpec((1, LANES), lambda bb, i: (0, 0)),
        ],
        out_specs=[
            pl.BlockSpec((1, cpb, nch, CHUNK + DN_DK, DN_DV), lambda bb, i: (bb, i, 0, 0, 0)),
            pl.BlockSpec((1, cpb, nch, CHUNK + DN_DK, DN_DV), lambda bb, i: (bb, i, 0, 0, 0)),
            pl.BlockSpec((1, cpb, nch, 8, LANES), lambda bb, i: (bb, i, 0, 0, 0)),
        ],
        out_shape=[
            jax.ShapeDtypeStruct((b, n, nch, CHUNK + DN_DK, DN_DV), BF16),
            jax.ShapeDtypeStruct((b, n, nch, CHUNK + DN_DK, DN_DV), BF16),
            jax.ShapeDtypeStruct((b, n, nch, 8, LANES), F32),
        ],
        scratch_shapes=[pltpu.VMEM((1, rows, w), F32), pltpu.VMEM((1, rows, w), F32),
                        pltpu.VMEM((1, rows, w), F32),
                        pltpu.VMEM((3, rows + 2 * SHORT_HALO, w), F32)],
        compiler_params=_params(("parallel", "parallel")),
        name="delta_prep",
    )(p3, p3, p3, p3, p3, p3, p3, p3, p3, short_w, p3, pad_row(a_log), pad_row(dt_bias))


def _delta_scan_kernel(prf, qof, glf, prb, qob, glb, of_ref, ob_ref, st_ref, *, cb):
    @pl.when(pl.program_id(1) == 0)
    def _():
        st_ref[...] = jnp.zeros_like(st_ref)

    for step in range(cb):
        for d, (pr_r, qo_r, gl_r, o_r) in enumerate(((prf, qof, glf, of_ref), (prb, qob, glb, ob_ref))):
            ci = step if d == 0 else cb - 1 - step
            for h in range(DN_HEADS):
                ch = d * DN_HEADS + h
                st = st_ref[ch]
                r = _dot(pr_r[0, ci, h], st.astype(BF16)) + qo_r[0, ci, h]
                o_r[0, ci * CHUNK:(ci + 1) * CHUNK, h * DN_DV:(h + 1) * DN_DV] = r[0:CHUNK]
                st_ref[ch] = st * gl_r[0, ci, h, 0:1, :] + r[CHUNK:]


def _delta_scan(pr, qo, gl, cb=2):
    b, n = pr.shape[0], pr.shape[1]
    s = n * CHUNK
    nb = n // cb
    hb = DN_HEADS

    def spec(shape, back):
        if back:
            return pl.BlockSpec(shape, lambda bb, i: (bb, nb - 1 - i, 1) + (0,) * (len(shape) - 3))
        return pl.BlockSpec(shape, lambda bb, i: (bb, i, 0) + (0,) * (len(shape) - 3))

    shapes = [(1, cb, hb, CHUNK + DN_DK, DN_DV), (1, cb, hb, CHUNK + DN_DK, DN_DV), (1, cb, hb, 8, LANES)]
    in_specs = [spec(sh, False) for sh in shapes] + [spec(sh, True) for sh in shapes]
    ow = hb * DN_DV
    return pl.pallas_call(
        functools.partial(_delta_scan_kernel, cb=cb),
        grid=(b, nb),
        in_specs=in_specs,
        out_specs=[pl.BlockSpec((1, cb * CHUNK, ow), lambda bb, i: (bb, i, 0)),
                   pl.BlockSpec((1, cb * CHUNK, ow), lambda bb, i: (bb, nb - 1 - i, 0))],
        out_shape=[jax.ShapeDtypeStruct((b, s, ow), F32)] * 2,
        scratch_shapes=[pltpu.VMEM((2 * hb, DN_DK, DN_DV), F32)],
        compiler_params=_params(("parallel", "arbitrary")),
        name="delta_scan",
    )(pr, qo, gl, pr, qo, gl)


def _even_out_kernel(cv_ref, cb_ref, lg_ref, lb_ref, of_ref, ob_ref, z_ref, og_ref, w_ref, x_ref, o_ref):
    a = cv_ref[...] + cb_ref[...]
    mu = jnp.mean(a, axis=-1, keepdims=True)
    xc = a - mu
    y = xc * lax.rsqrt(jnp.mean(xc * xc, axis=-1, keepdims=True) + EPS)
    acc = x_ref[...] + _dot(_silu(y * lg_ref[...] + lb_ref[...]).astype(BF16), w_ref[0:CONV_CH, :])
    gated = []
    for h in range(DN_HEADS):
        sl = slice(h * DN_DV, (h + 1) * DN_DV)
        o = of_ref[:, sl] + ob_ref[:, sl]
        gated.append((_rms(o, og_ref[...]) * _silu(z_ref[:, sl])).astype(BF16))
    o_ref[...] = acc + _dot(jnp.concatenate(gated, axis=1), w_ref[CONV_CH:, :])


def _even_out(conv, conv_b, ln_g, ln_b, o_f, o_b, p, onorm_g, w_out, x):
    t, d = x.shape
    tm = min(256, t)
    kk = CONV_CH + DN_HEADS * DN_DV
    zblk = (2 * CONV_CH + 3 * DN_HEADS * DN_DK) // CONV_CH

    def row(n):
        return pl.BlockSpec((1, n), lambda i: (0, 0))

    def tile(n, jblk=0):
        return pl.BlockSpec((tm, n), lambda i: (i, jblk))

    return pl.pallas_call(
        _even_out_kernel,
        grid=(t // tm,),
        in_specs=[tile(CONV_CH), row(CONV_CH), row(CONV_CH), row(CONV_CH),
                  tile(CONV_CH), tile(CONV_CH), tile(CONV_CH, zblk), row(DN_DV),
                  pl.BlockSpec((kk, d), lambda i: (0, 0)),
                  tile(d)],
        out_specs=tile(d),
        out_shape=jax.ShapeDtypeStruct((t, d), F32),
        compiler_params=_params(("parallel",)),
        name="even_out",
    )(conv, conv_b.reshape(1, -1), ln_g.reshape(1, -1), ln_b.reshape(1, -1), o_f, o_b, p,
      onorm_g.reshape(1, -1), w_out, x)


def _even_mixer(x, b, s, norm_g, w_in, conv_w, conv_b, ln_g, ln_b, short_w, a_log, dt_bias,
                onorm_g, w_out):
    t = b * s
    p = _norm_matmul(x, norm_g, w_in, tn=5 * MXU_WIDTH)
    p3 = p.reshape(b, s, EVEN_IN_PAD)
    ts = min(512, s)
    conv = _glu_conv(p3, conv_w, ts)
    pr, qo, gl = _delta_prep(p3, short_w, a_log, dt_bias)
    o_f, o_b = _delta_scan(pr, qo, gl)
    return _even_out(conv.reshape(t, -1), conv_b, ln_g, ln_b, o_f.reshape(t, -1), o_b.reshape(t, -1),
                     p, onorm_g, w_out, x)


def _qkv_rope_kernel(xin_ref, g_ref, w_ref, pos_ref, freq_ref, o_ref, xn_ref, cs_ref):
    j = pl.program_id(1)
    nrep = o_ref.shape[1] // LANES

    @pl.when(j == 0)
    def _():
        xn_ref[...] = _rms(xin_ref[...], g_ref[...]).astype(BF16)
        ang = pos_ref[...].astype(F32) * freq_ref[...]
        lane = lax.broadcasted_iota(jnp.int32, ang.shape, 1)
        half = ROPE_DIMS // 2
        cs_ref[0] = jnp.where(lane < ROPE_DIMS, jnp.cos(ang), 1.0)
        sn = jnp.sin(ang)
        cs_ref[1] = jnp.where(lane < half, -sn, 0.0)
        cs_ref[2] = jnp.where((lane >= half) & (lane < ROPE_DIMS), sn, 0.0)

    nq = DA_HEADS * 2 * DA_DH // o_ref.shape[1]

    rot = j < 2 * nq
    half = ROPE_DIMS // 2
    w = o_ref.shape[1] // 2
    qs = jnp.where(j < nq, DA_DH ** -0.5 * math.log2(math.e), 1.0)
    c = jnp.concatenate([jnp.where(rot, cs_ref[0], 1.0) * qs] * (nrep // 2), axis=1)
    s1 = jnp.concatenate([jnp.where(rot, cs_ref[1], 0.0) * qs] * (nrep // 2), axis=1)
    s2 = jnp.concatenate([jnp.where(rot, cs_ref[2], 0.0) * qs] * (nrep // 2), axis=1)
    xn = xn_ref[...]
    for part in range(2):
        x = _dot(xn, w_ref[:, part * w:(part + 1) * w])
        o_ref[:, part * w:(part + 1) * w] = (
            x * c + pltpu.roll(x, w - half, 1) * s1 + pltpu.roll(x, half, 1) * s2).astype(BF16)


def _qkv_rope(x, g, w, pos, freq):
    t, d = x.shape
    n = w.shape[1]
    tm = min(1024, t)
    tn = 4 * MXU_WIDTH
    return pl.pallas_call(
        _qkv_rope_kernel,
        grid=(t // tm, n // tn),
        in_specs=[pl.BlockSpec((tm, d), lambda i, j: (i, 0)),
                  pl.BlockSpec((1, d), lambda i, j: (0, 0)),
                  pl.BlockSpec((d, tn), lambda i, j: (0, j)),
                  pl.BlockSpec((tm, 1), lambda i, j: (i, 0)),
                  pl.BlockSpec((1, LANES), lambda i, j: (0, 0))],
        out_specs=pl.BlockSpec((tm, tn), lambda i, j: (i, j)),
        out_shape=jax.ShapeDtypeStruct((t, n), BF16),
        scratch_shapes=[pltpu.VMEM((tm, d), BF16), pltpu.VMEM((3, tm, LANES), F32)],
        compiler_params=_params(("parallel", "arbitrary")),
        name="qkv_rope",
    )(x, g.reshape(1, d), w, pos, freq)


def _attn_kernel(q_ref, k_ref, v_ref, lq1, lk1, lq2, lk2, sg_ref, o_ref, *, lambda_init, nsplit):
    lam = (jnp.exp(jnp.sum(lq1[...] * lk1[...], axis=-1, keepdims=True))
           - jnp.exp(jnp.sum(lq2[...] * lk2[...], axis=-1, keepdims=True)) + lambda_init)
    k = k_ref[...]
    v = v_ref[...]
    rows = q_ref.shape[0] // nsplit
    for part in range(nsplit):
        rs = slice(part * rows, (part + 1) * rows)
        q = q_ref[rs, :]
        s1 = _dot_nt(q[:, 0:DA_DH], k[:, 0:DA_DH])
        s2 = _dot_nt(q[:, DA_DH:], k[:, DA_DH:])
        e1 = jnp.exp2(s1 - jnp.max(s1, axis=-1, keepdims=True))
        e2 = jnp.exp2(s2 - jnp.max(s2, axis=-1, keepdims=True))
        r1 = 1.0 / jnp.sum(e1, axis=-1, keepdims=True)
        r2 = lam / jnp.sum(e2, axis=-1, keepdims=True)
        o = _dot(e1.astype(BF16), v) * r1 - _dot(e2.astype(BF16), v) * r2
        o_ref[rs, :] = (_rms(o, sg_ref[...]) * (1.0 - lambda_init)).astype(BF16)


def _diff_attention(qkvb, b, s, lq1, lk1, lq2, lk2, subln_g, lambda_init):
    t = b * s
    hw = 2 * DA_DH
    tq = min(1024, s)
    nq = s // tq

    def row(n):
        return pl.BlockSpec((1, n), lambda bb, h, i: (0, 0))

    return pl.pallas_call(
        functools.partial(_attn_kernel, lambda_init=lambda_init, nsplit=max(1, tq // 128)),
        grid=(b, DA_HEADS, nq),
        in_specs=[pl.BlockSpec((tq, hw), lambda bb, h, i: (bb * nq + i, h)),
                  pl.BlockSpec((s, hw), lambda bb, h, i: (bb, DA_HEADS + h)),
                  pl.BlockSpec((s, hw), lambda bb, h, i: (bb, 2 * DA_HEADS + h)),
                  row(DA_DH), row(DA_DH), row(DA_DH), row(DA_DH), row(hw)],
        out_specs=pl.BlockSpec((tq, hw), lambda bb, h, i: (bb * nq + i, h)),
        out_shape=jax.ShapeDtypeStruct((t, DA_HEADS * hw), BF16),
        compiler_params=_params(("parallel", "parallel", "arbitrary")),
        name="diff_attention",
    )(qkvb, qkvb, qkvb, lq1.reshape(1, -1), lk1.reshape(1, -1), lq2.reshape(1, -1),
      lk2.reshape(1, -1), subln_g.reshape(1, -1))


def _mm_res_kernel(a_ref, w_ref, x_ref, o_ref):
    o_ref[...] = x_ref[...] + _dot(a_ref[...], w_ref[...])


def _matmul_residual(a, w, x):
    t, d = x.shape
    kk = a.shape[1]
    tm = min(512, t)
    tn = 1024
    return pl.pallas_call(
        _mm_res_kernel,
        grid=(t // tm, d // tn),
        in_specs=[pl.BlockSpec((tm, kk), lambda i, j: (i, 0)),
                  pl.BlockSpec((kk, tn), lambda i, j: (0, j)),
                  pl.BlockSpec((tm, tn), lambda i, j: (i, j))],
        out_specs=pl.BlockSpec((tm, tn), lambda i, j: (i, j)),
        out_shape=jax.ShapeDtypeStruct((t, d), F32),
        compiler_params=_params(("parallel", "arbitrary")),
        name="matmul_residual",
    )(a, w, x)


def _odd_mixer(x, b, s, positions, norm_g, w_qkv, lq1, lk1, lq2, lk2, subln_g, w_o, lambda_init):
    t = b * s
    inv_freq = ROPE_THETA ** (-jnp.arange(0, ROPE_DIMS, 2, dtype=F32) / ROPE_DIMS)
    freq = jnp.concatenate([inv_freq, inv_freq, jnp.zeros((LANES - ROPE_DIMS,), F32)]).reshape(1, LANES)
    qkvb = _qkv_rope(x, norm_g, w_qkv, positions.reshape(t, 1), freq)
    attn = _diff_attention(qkvb, b, s, lq1, lk1, lq2, lk2, subln_g, lambda_init)
    return _matmul_residual(attn, w_o, x)


def kernel(x, positions, norm_ffn1, ffn1_wg, ffn1_wu, ffn1_wd, norm_mix, norm_ffn2, ffn2_wg, ffn2_wu, ffn2_wd, ev_w_in, ev_conv_w, ev_conv_b, ev_ln_g, ev_ln_b, ev_short_w, ev_a_log, ev_dt_bias, ev_onorm_g, ev_w_out, od_w_qkv, od_lq1, od_lk1, od_lq2, od_lk2, od_subln_g, od_w_o, final_norm):
    b, s, d = x.shape
    depth = norm_ffn1.shape[0]
    xf = x.reshape(b * s, d)
    for i in range(depth):
        xf = _ffn(xf, norm_ffn1[i], ffn1_wg, ffn1_wu, ffn1_wd, i)
        j = i // 2
        if i % 2 == 0:
            xf = _even_mixer(xf, b, s, norm_mix[i], _cast_bf16(ev_w_in, j, EVEN_IN_PAD), ev_conv_w[j],
                             ev_conv_b[j], ev_ln_g[j],
                             ev_ln_b[j], ev_short_w[j], ev_a_log[j], ev_dt_bias[j], ev_onorm_g[j],
                             _cast_bf16(ev_w_out, j))
        else:
            lambda_init = 0.8 - 0.6 * math.exp(-0.3 * i)
            xf = _odd_mixer(xf, b, s, positions, norm_mix[i], _cast_bf16(od_w_qkv, j), od_lq1[j], od_lk1[j],
                            od_lq2[j], od_lk2[j], od_subln_g[j], _cast_bf16(od_w_o, j), lambda_init)
        xf = _ffn(xf, norm_ffn2[i], ffn2_wg, ffn2_wu, ffn2_wd, i,
                  final_g=final_norm if i == depth - 1 else None)
    return xf.reshape(b, s, d)
```

```python
import functools
import math

import jax
import jax.numpy as jnp
from jax import lax
from jax.experimental import pallas as pl
from jax.experimental.pallas import tpu as pltpu

F32 = jnp.float32
BF16 = jnp.bfloat16
EPS = 1e-6

D_FF = 5632
CONV_CH = 1024
CONV_WIDTH = 31
DN_HEADS = 8
DN_DK = 128
DN_DV = 128
SHORT_CONV = 3
CHUNK = 64
DA_HEADS = 8
DA_DH = 128
ROPE_THETA = 500000.0
ROPE_DIMS = DA_DH // 4

LANES = 128
CONV_HALO = 16
SHORT_HALO = 8
MXU_WIDTH = 256
EVEN_IN_PAD = 6400
SCAL_COL_BLOCK = 6144 // LANES
VMEM_LIMIT = 56 * 1024 * 1024
HIGHEST = lax.Precision.HIGHEST


def _sigmoid(x):
    return 1.0 / (1.0 + jnp.exp(-x))


def _silu(x):
    return x * _sigmoid(x)


def _softplus(x):
    return jnp.maximum(x, 0.0) + jnp.log(1.0 + jnp.exp(-jnp.abs(x)))


def _rms(x, g):
    ms = jnp.mean(x * x, axis=-1, keepdims=True)
    return x * lax.rsqrt(ms + EPS) * g


def _dot(a, b):
    return jnp.dot(a, b, preferred_element_type=F32)


def _dot_nt(a, b, precision=None):
    return lax.dot_general(a, b, (((1,), (1,)), ((), ())), precision=precision,
                           preferred_element_type=F32)


def _split3(x):
    hi = x.astype(BF16)
    r1 = x - hi.astype(F32)
    mid = r1.astype(BF16)
    lo = (r1 - mid.astype(F32)).astype(BF16)
    return hi, mid, lo


def _params(sem):
    return pltpu.CompilerParams(dimension_semantics=sem, vmem_limit_bytes=VMEM_LIMIT)


def _cast_kernel(w_ref, o_ref):
    cols = w_ref.shape[1]
    o_ref[:, 0:cols] = w_ref[...].astype(BF16)
    if o_ref.shape[1] > cols:
        o_ref[:, cols:] = jnp.zeros((o_ref.shape[0], o_ref.shape[1] - cols), BF16)


def _cast_bf16(w, layer, pad_cols=None):
    rows, cols = w.shape[-2:]
    ocols = cols if pad_cols is None else pad_cols
    nblk = 8
    tr = rows // nblk
    return pl.pallas_call(
        _cast_kernel,
        grid=(nblk,),
        in_specs=[pl.BlockSpec((None, tr, cols), lambda r: (layer, r, 0))],
        out_specs=pl.BlockSpec((tr, ocols), lambda r: (r, 0)),
        out_shape=jax.ShapeDtypeStruct((rows, ocols), BF16),
        compiler_params=_params(("parallel",)),
        name="cast_bf16",
    )(w)


def _ffn_kernel(x_ref, g_ref, wg_ref, wu_ref, wd_ref, *rest, final):
    if final:
        fg_ref, o_ref, xn_ref, acc_ref = rest
    else:
        o_ref, xn_ref, acc_ref = rest
    j = pl.program_id(1)

    @pl.when(j == 0)
    def _():
        xn_ref[...] = _rms(x_ref[...], g_ref[...]).astype(BF16)
        acc_ref[...] = jnp.zeros_like(acc_ref)

    xn = xn_ref[...]
    a = _dot(xn, wg_ref[...])
    b = _dot(xn, wu_ref[...])
    h = (_silu(a) * b).astype(BF16)
    acc_ref[...] += _dot(h, wd_ref[...])

    @pl.when(j == pl.num_programs(1) - 1)
    def _():
        y = x_ref[...] + 0.5 * acc_ref[...]
        if final:
            y = _rms(y, fg_ref[...])
        o_ref[...] = y


def _ffn(x, g, wg, wu, wd, layer, final_g=None):
    t, d = x.shape
    ff = wg.shape[-1]
    tm = min(512, t)
    tf = 512
    final = final_g is not None
    in_specs = [
        pl.BlockSpec((tm, d), lambda i, j: (i, 0)),
        pl.BlockSpec((1, d), lambda i, j: (0, 0)),
        pl.BlockSpec((d, tf), lambda i, j: (0, j)),
        pl.BlockSpec((d, tf), lambda i, j: (0, j)),
        pl.BlockSpec((tf, d), lambda i, j: (j, 0)),
    ]
    args = [x, g.reshape(1, d), _cast_bf16(wg, layer), _cast_bf16(wu, layer), _cast_bf16(wd, layer)]
    if final:
        in_specs.append(pl.BlockSpec((1, d), lambda i, j: (0, 0)))
        args.append(final_g.reshape(1, d))
    return pl.pallas_call(
        functools.partial(_ffn_kernel, final=final),
        grid=(t // tm, ff // tf),
        in_specs=in_specs,
        out_specs=pl.BlockSpec((tm, d), lambda i, j: (i, 0)),
        out_shape=jax.ShapeDtypeStruct((t, d), F32),
        scratch_shapes=[pltpu.VMEM((tm, d), BF16), pltpu.VMEM((tm, d), F32)],
        compiler_params=_params(("parallel", "arbitrary")),
        name="ffn",
    )(*args)


def _norm_mm_kernel(x_ref, g_ref, w_ref, o_ref, xn_ref):
    @pl.when(pl.program_id(1) == 0)
    def _():
        xn_ref[...] = _rms(x_ref[...], g_ref[...]).astype(BF16)

    o_ref[...] = _dot(xn_ref[...], w_ref[...])


def _norm_matmul(x, g, w, tn):
    t, d = x.shape
    n = w.shape[1]
    tm = min(1024, t)
    return pl.pallas_call(
        _norm_mm_kernel,
        grid=(t // tm, n // tn),
        in_specs=[
            pl.BlockSpec((tm, d), lambda i, j: (i, 0)),
            pl.BlockSpec((1, d), lambda i, j: (0, 0)),
            pl.BlockSpec((d, tn), lambda i, j: (0, j)),
        ],
        out_specs=pl.BlockSpec((tm, tn), lambda i, j: (i, j)),
        out_shape=jax.ShapeDtypeStruct((t, n), F32),
        scratch_shapes=[pltpu.VMEM((tm, d), BF16)],
        compiler_params=_params(("parallel", "arbitrary")),
        name="norm_matmul",
    )(x, g.reshape(1, d), w)


def _delta_prep_kernel(pq, pk, pv, hqp, hkp, hvp, hqn, hkn, hvn, w_ref, sc_ref, alog_ref, dtb_ref,
                       pr_ref, qo_ref, gl_ref, q_ref, k_ref, v_ref, scr, *, cpb):
    i = pl.program_id(1)
    last = pl.num_programs(1) - 1
    nrow = cpb * CHUNK
    w = DN_HEADS * DN_DK
    for idx, (main, hp, hn, dst) in enumerate(((pq, hqp, hqn, q_ref), (pk, hkp, hkn, k_ref),
                                               (pv, hvp, hvn, v_ref))):
        scr[idx, 0:SHORT_HALO, :] = jnp.where(i > 0, hp[0], 0.0)
        scr[idx, SHORT_HALO:SHORT_HALO + nrow, :] = main[0]
        scr[idx, SHORT_HALO + nrow:2 * SHORT_HALO + nrow, :] = jnp.where(i < last, hn[0], 0.0)
        ws = w_ref[:, idx * w:(idx + 1) * w]
        y = (ws[0:1, :] * scr[idx, SHORT_HALO - 1:SHORT_HALO - 1 + nrow, :]
             + ws[1:2, :] * scr[idx, SHORT_HALO:SHORT_HALO + nrow, :]
             + ws[2:3, :] * scr[idx, SHORT_HALO + 1:SHORT_HALO + 1 + nrow, :])
        y = _silu(y)
        if idx == 2:
            dst[0] = y
        else:
            scale = DN_DK ** -0.5 if idx == 0 else 1.0
            for h in range(DN_HEADS):
                sl = slice(h * DN_DK, (h + 1) * DN_DK)
                yh = y[:, sl]
                dst[0, :, sl] = yh * (lax.rsqrt(jnp.sum(yh * yh, axis=-1, keepdims=True) + EPS) * scale)

    for ci in range(cpb):
        _delta_prep_chunk(ci, q_ref, k_ref, v_ref, sc_ref, alog_ref, dtb_ref, pr_ref, qo_ref, gl_ref)


def _delta_prep_chunk(ci, q_ref, k_ref, v_ref, sc_ref, alog_ref, dtb_ref, pr_ref, qo_ref, gl_ref):
    rows = slice(ci * CHUNK, (ci + 1) * CHUNK)
    sc = sc_ref[0, rows, :]
    beta = _sigmoid(sc)
    araw = pltpu.roll(sc, LANES - 2 * DN_HEADS, 1)
    g = -jnp.exp(alog_ref[...]) * _softplus(araw + dtb_ref[...])

    ii = lax.broadcasted_iota(jnp.int32, (CHUNK, CHUNK), 0)
    jj = lax.broadcasted_iota(jnp.int32, (CHUNK, CHUNK), 1)
    tri = jnp.concatenate([(ii >= jj).astype(BF16), (ii <= jj).astype(BF16)], axis=0)
    gsum = _dot(tri, jnp.concatenate(_split3(g), axis=1))
    gsum = gsum[:, 0:LANES] + gsum[:, LANES:2 * LANES] + gsum[:, 2 * LANES:]
    gcf = gsum[0:CHUNK]
    gcb = gsum[CHUNK:]
    lane = lax.broadcasted_iota(jnp.int32, (CHUNK, LANES), 1)
    gc = jnp.where(lane < DN_HEADS, gcf, gcb)
    gtot = jnp.broadcast_to(gcf[CHUNK - 1:CHUNK, :], (CHUNK, LANES))
    gexp = jnp.exp(gc)
    kdec = jnp.exp(gtot - gc)
    gl = jnp.exp(gtot)

    nch = 2 * DN_HEADS
    wcat = nch * CHUNK
    er = lax.broadcasted_iota(jnp.int32, (LANES, wcat), 0)
    el = lax.broadcasted_iota(jnp.int32, (LANES, wcat), 1)
    expand = (jnp.right_shift(el, 6) == er).astype(BF16)
    bhi, bmid, _ = _split3(beta)
    spread = _dot(jnp.concatenate(_split3(gc) + (bhi, bmid), axis=0), expand)
    gcat = spread[0:CHUNK] + spread[CHUNK:2 * CHUNK] + spread[2 * CHUNK:3 * CHUNK]
    bcat = spread[3 * CHUNK:4 * CHUNK] + spread[4 * CHUNK:]
    tt = lax.broadcasted_iota(jnp.int32, (CHUNK, wcat), 0)
    ll = lax.broadcasted_iota(jnp.int32, (CHUNK, wcat), 1)
    jl = jnp.bitwise_and(ll, CHUNK - 1)
    diag = tt == jl
    rowcat = jnp.broadcast_to(jnp.sum(jnp.where(diag, gcat, 0.0), axis=0, keepdims=True),
                              (CHUNK, wcat))
    ahead = jnp.where(ll < DN_HEADS * CHUNK, tt - jl, jl - tt)
    incl = ahead >= 0
    strict = ahead > 0
    dec = jnp.exp(jnp.where(incl, gcat - rowcat, -jnp.inf))

    kks, qks, ks, qs, vs = [], [], [], [], []
    for h in range(DN_HEADS):
        sl = slice(h * DN_DK, (h + 1) * DN_DK)
        qh = q_ref[0, rows, sl]
        kh = k_ref[0, rows, sl]
        khb = kh.astype(BF16)
        gram = _dot_nt(jnp.concatenate([khb, qh.astype(BF16)], axis=0), khb)
        kks.append(gram[0:CHUNK])
        qks.append(gram[CHUNK:])
        ks.append(kh)
        qs.append(qh)
        vs.append(v_ref[0, rows, sl])
    kkcat = jnp.concatenate(kks + kks, axis=1)
    qkcat = jnp.concatenate(qks + qks, axis=1)
    mcat = jnp.where(strict, -(kkcat * bcat) * dec, 0.0)
    acat = jnp.where(incl, qkcat * dec, 0.0)

    gw = 4 * CHUNK
    br = lax.broadcasted_iota(jnp.int32, (gw, gw), 0)
    bc = lax.broadcasted_iota(jnp.int32, (gw, gw), 1)
    bmask = jnp.right_shift(br, 6) == jnp.right_shift(bc, 6)

    def blockdiag(xg):
        xb = xg.astype(BF16)
        return jnp.where(bmask, jnp.concatenate([xb] * 4, axis=0), jnp.zeros((), BF16))

    ngrp = wcat // gw
    xs = [mcat[:, g * gw:(g + 1) * gw] for g in range(ngrp)]
    tinv = [jnp.where(diag[:, g * gw:(g + 1) * gw], 1.0, 0.0) + xs[g] for g in range(ngrp)]
    xbd = [blockdiag(x) for x in xs]
    for _ in range(5):
        xs = [_dot(xs[g].astype(BF16), xbd[g]) for g in range(ngrp)]
        xbd = [blockdiag(x) for x in xs]
        tinv = [tinv[g] + _dot(tinv[g].astype(BF16), xbd[g]) for g in range(ngrp)]

    rhs, kds, qds = [], [], []
    for c in range(nch):
        h = c % DN_HEADS
        bcol = beta[:, c:c + 1]
        gex = gexp[:, c:c + 1]
        rhs.append(jnp.concatenate([vs[h] * bcol, ks[h] * (bcol * gex)], axis=1).astype(BF16))
        kds.append((ks[h] * kdec[:, c:c + 1]).astype(BF16))
        qds.append(qs[h] * gex)
    uws = []
    for c in range(nch):
        g, o = divmod(c, 4)
        tc = tinv[g][:, o * CHUNK:(o + 1) * CHUNK]
        uws.append(_dot(tc.astype(BF16), rhs[c]).astype(BF16))
    auw = [_dot(acat[:, c * CHUNK:(c + 1) * CHUNK].astype(BF16), uws[c]) for c in range(nch)]
    kuw = [lax.dot_general(kds[c], uws[c], (((0,), (0,)), ((), ())), preferred_element_type=F32)
           for c in range(nch)]
    for c in range(nch):
        pr_ref[0, ci, c] = jnp.concatenate([qds[c] - auw[c][:, DN_DV:], -kuw[c][:, DN_DV:]],
                                           axis=0).astype(BF16)
        qo_ref[0, ci, c] = jnp.concatenate([auw[c][:, 0:DN_DV], kuw[c][:, 0:DN_DV]], axis=0).astype(BF16)
        gl_ref[0, ci, c] = jnp.broadcast_to(gl[0:8, c:c + 1], (8, LANES))


def _delta_prep(p3, short_w, a_log, dt_bias):
    b, s, _ = p3.shape
    n = s // CHUNK
    nch = 2 * DN_HEADS
    w = DN_HEADS * DN_DK
    off = 2 * CONV_CH // w

    def pad_row(a):
        return jnp.pad(a.reshape(1, nch), ((0, 0), (0, LANES - nch)))

    cpb = 2
    rows = cpb * CHUNK
    r = rows // SHORT_HALO
    nh = s // SHORT_HALO

    def main(c):
        return pl.BlockSpec((1, rows, w), lambda bb, i: (bb, i, off + c))

    def prev(c):
        return pl.BlockSpec((1, SHORT_HALO, w), lambda bb, i: (bb, jnp.maximum(i * r - 1, 0), off + c))

    def nxt(c):
        return pl.BlockSpec((1, SHORT_HALO, w), lambda bb, i: (bb, jnp.minimum((i + 1) * r, nh - 1), off + c))

    return pl.pallas_call(
        functools.partial(_delta_prep_kernel, cpb=cpb),
        grid=(b, n // cpb),
        in_specs=[
            main(0), main(1), main(2), prev(0), prev(1), prev(2), nxt(0), nxt(1), nxt(2),
            pl.BlockSpec((SHORT_CONV, 3 * w), lambda bb, i: (0, 0)),
            pl.BlockSpec((1, rows, LANES), lambda bb, i: (bb, i, SCAL_COL_BLOCK)),
            pl.BlockSpec((1, LANES), lambda bb, i: (0, 0)),
            pl.BlockSpec((1, LANES), lambda bb, i: (0, 0)),
        ],
        out_specs=[
            pl.BlockSpec((1, cpb, nch, CHUNK + DN_DK, DN_DV), lambda bb, i: (bb, i, 0, 0, 0)),
            pl.BlockSpec((1, cpb, nch, CHUNK + DN_DK, DN_DV), lambda bb, i: (bb, i, 0, 0, 0)),
            pl.BlockSpec((1, cpb, nch, 8, LANES), lambda bb, i: (bb, i, 0, 0, 0)),
        ],
        out_shape=[
            jax.ShapeDtypeStruct((b, n, nch, CHUNK + DN_DK, DN_DV), BF16),
            jax.ShapeDtypeStruct((b, n, nch, CHUNK + DN_DK, DN_DV), BF16),
            jax.ShapeDtypeStruct((b, n, nch, 8, LANES), F32),
        ],
        scratch_shapes=[pltpu.VMEM((1, rows, w), F32), pltpu.VMEM((1, rows, w), F32),
                        pltpu.VMEM((1, rows, w), F32),
                        pltpu.VMEM((3, rows + 2 * SHORT_HALO, w), F32)],
        compiler_params=_params(("parallel", "parallel")),
        name="delta_prep",
    )(p3, p3, p3, p3, p3, p3, p3, p3, p3, short_w, p3, pad_row(a_log), pad_row(dt_bias))


def _delta_scan_kernel(prf, qof, glf, prb, qob, glb, of_ref, ob_ref, st_ref, *, cb):
    @pl.when(pl.program_id(1) == 0)
    def _():
        st_ref[...] = jnp.zeros_like(st_ref)

    for step in range(cb):
        for d, (pr_r, qo_r, gl_r, o_r) in enumerate(((prf, qof, glf, of_ref), (prb, qob, glb, ob_ref))):
            ci = step if d == 0 else cb - 1 - step
            for h in range(DN_HEADS):
                ch = d * DN_HEADS + h
                st = st_ref[ch]
                r = _dot(pr_r[0, ci, h], st.astype(BF16)) + qo_r[0, ci, h]
                o_r[0, ci * CHUNK:(ci + 1) * CHUNK, h * DN_DV:(h + 1) * DN_DV] = r[0:CHUNK]
                st_ref[ch] = st * gl_r[0, ci, h, 0:1, :] + r[CHUNK:]


def _delta_scan(pr, qo, gl, cb=2):
    b, n = pr.shape[0], pr.shape[1]
    s = n * CHUNK
    nb = n // cb
    hb = DN_HEADS

    def spec(shape, back):
        if back:
            return pl.BlockSpec(shape, lambda bb, i: (bb, nb - 1 - i, 1) + (0,) * (len(shape) - 3))
        return pl.BlockSpec(shape, lambda bb, i: (bb, i, 0) + (0,) * (len(shape) - 3))

    shapes = [(1, cb, hb, CHUNK + DN_DK, DN_DV), (1, cb, hb, CHUNK + DN_DK, DN_DV), (1, cb, hb, 8, LANES)]
    in_specs = [spec(sh, False) for sh in shapes] + [spec(sh, True) for sh in shapes]
    ow = hb * DN_DV
    return pl.pallas_call(
        functools.partial(_delta_scan_kernel, cb=cb),
        grid=(b, nb),
        in_specs=in_specs,
        out_specs=[pl.BlockSpec((1, cb * CHUNK, ow), lambda bb, i: (bb, i, 0)),
                   pl.BlockSpec((1, cb * CHUNK, ow), lambda bb, i: (bb, nb - 1 - i, 0))],
        out_shape=[jax.ShapeDtypeStruct((b, s, ow), F32)] * 2,
        scratch_shapes=[pltpu.VMEM((2 * hb, DN_DK, DN_DV), F32)],
        compiler_params=_params(("parallel", "arbitrary")),
        name="delta_scan",
    )(pr, qo, gl, pr, qo, gl)


def _even_out_kernel(pv, pg, hvp, hgp, hvn, hgn, cw_ref, cb_ref, lg_ref, lb_ref, of_ref, ob_ref, z_ref,
                     og_ref, w_ref, x_ref, o_ref, scr, *, tps):
    tm = pv.shape[0]
    pos = lax.rem(pl.program_id(0), tps)
    head = jnp.where(pos > 0, hvp[...] * _sigmoid(hgp[...]), 0.0)
    body = pv[...] * _sigmoid(pg[...])
    tail = jnp.where(pos < tps - 1, hvn[...] * _sigmoid(hgn[...]), 0.0)
    convs = []
    for c in range(CONV_CH // LANES):
        cs = slice(c * LANES, (c + 1) * LANES)
        scr[c, 0:CONV_HALO, :] = head[:, cs]
        scr[c, CONV_HALO:CONV_HALO + tm, :] = body[:, cs]
        scr[c, CONV_HALO + tm:2 * CONV_HALO + tm, :] = tail[:, cs]
        acc = cw_ref[0:1, cs] * scr[c, 1:1 + tm, :]
        for k in range(1, CONV_WIDTH):
            acc = acc + cw_ref[k:k + 1, cs] * scr[c, 1 + k:1 + k + tm, :]
        convs.append(acc)
    a = jnp.concatenate(convs, axis=1) + cb_ref[...]
    mu = jnp.mean(a, axis=-1, keepdims=True)
    xc = a - mu
    y = xc * lax.rsqrt(jnp.mean(xc * xc, axis=-1, keepdims=True) + EPS)
    acc = x_ref[...] + _dot(_silu(y * lg_ref[...] + lb_ref[...]).astype(BF16), w_ref[0:CONV_CH, :])
    gated = []
    for h in range(DN_HEADS):
        sl = slice(h * DN_DV, (h + 1) * DN_DV)
        o = of_ref[:, sl] + ob_ref[:, sl]
        gated.append((_rms(o, og_ref[...]) * _silu(z_ref[:, sl])).astype(BF16))
    o_ref[...] = acc + _dot(jnp.concatenate(gated, axis=1), w_ref[CONV_CH:, :])


def _even_out(s, conv_w, conv_b, ln_g, ln_b, o_f, o_b, p, onorm_g, w_out, x):
    t, d = x.shape
    tm = min(256, s)
    kk = CONV_CH + DN_HEADS * DN_DV
    zblk = (2 * CONV_CH + 3 * DN_HEADS * DN_DK) // CONV_CH
    r = tm // CONV_HALO
    nh = t // CONV_HALO

    def row(n):
        return pl.BlockSpec((1, n), lambda i: (0, 0))

    def tile(n, jblk=0):
        return pl.BlockSpec((tm, n), lambda i: (i, jblk))

    def prev(jblk):
        return pl.BlockSpec((CONV_HALO, CONV_CH), lambda i: (jnp.maximum(i * r - 1, 0), jblk))

    def nxt(jblk):
        return pl.BlockSpec((CONV_HALO, CONV_CH), lambda i: (jnp.minimum((i + 1) * r, nh - 1), jblk))

    return pl.pallas_call(
        functools.partial(_even_out_kernel, tps=s // tm),
        grid=(t // tm,),
        in_specs=[tile(CONV_CH, 0), tile(CONV_CH, 1), prev(0), prev(1), nxt(0), nxt(1),
                  pl.BlockSpec((CONV_WIDTH, CONV_CH), lambda i: (0, 0)),
                  row(CONV_CH), row(CONV_CH), row(CONV_CH),
                  tile(CONV_CH), tile(CONV_CH), tile(CONV_CH, zblk), row(DN_DV),
                  pl.BlockSpec((kk, d), lambda i: (0, 0)),
                  tile(d)],
        out_specs=tile(d),
        out_shape=jax.ShapeDtypeStruct((t, d), F32),
        scratch_shapes=[pltpu.VMEM((CONV_CH // LANES, tm + 2 * CONV_HALO, LANES), F32)],
        compiler_params=_params(("parallel",)),
        name="even_out",
    )(p, p, p, p, p, p, conv_w, conv_b.reshape(1, -1), ln_g.reshape(1, -1), ln_b.reshape(1, -1), o_f, o_b, p,
      onorm_g.reshape(1, -1), w_out, x)


def _even_mixer(x, b, s, norm_g, w_in, conv_w, conv_b, ln_g, ln_b, short_w, a_log, dt_bias,
                onorm_g, w_out):
    t = b * s
    p = _norm_matmul(x, norm_g, w_in, tn=5 * MXU_WIDTH)
    p3 = p.reshape(b, s, EVEN_IN_PAD)
    pr, qo, gl = _delta_prep(p3, short_w, a_log, dt_bias)
    o_f, o_b = _delta_scan(pr, qo, gl)
    return _even_out(s, conv_w, conv_b, ln_g, ln_b, o_f.reshape(t, -1), o_b.reshape(t, -1),
                     p, onorm_g, w_out, x)


def _qkv_rope_kernel(xin_ref, g_ref, w_ref, pos_ref, freq_ref, o_ref, xn_ref, cs_ref):
    j = pl.program_id(1)
    nrep = o_ref.shape[1] // LANES

    @pl.when(j == 0)
    def _():
        xn_ref[...] = _rms(xin_ref[...], g_ref[...]).astype(BF16)
        ang = pos_ref[...].astype(F32) * freq_ref[...]
        lane = lax.broadcasted_iota(jnp.int32, ang.shape, 1)
        half = ROPE_DIMS // 2
        cs_ref[0] = jnp.where(lane < ROPE_DIMS, jnp.cos(ang), 1.0)
        sn = jnp.sin(ang)
        cs_ref[1] = jnp.where(lane < half, -sn, 0.0)
        cs_ref[2] = jnp.where((lane >= half) & (lane < ROPE_DIMS), sn, 0.0)

    nq = DA_HEADS * 2 * DA_DH // o_ref.shape[1]

    rot = j < 2 * nq
    half = ROPE_DIMS // 2
    w = o_ref.shape[1] // 2
    qs = jnp.where(j < nq, DA_DH ** -0.5 * math.log2(math.e), 1.0)
    c = jnp.concatenate([jnp.where(rot, cs_ref[0], 1.0) * qs] * (nrep // 2), axis=1)
    s1 = jnp.concatenate([jnp.where(rot, cs_ref[1], 0.0) * qs] * (nrep // 2), axis=1)
    s2 = jnp.concatenate([jnp.where(rot, cs_ref[2], 0.0) * qs] * (nrep // 2), axis=1)
    xn = xn_ref[...]
    for part in range(2):
        x = _dot(xn, w_ref[:, part * w:(part + 1) * w])
        o_ref[:, part * w:(part + 1) * w] = (
            x * c + pltpu.roll(x, w - half, 1) * s1 + pltpu.roll(x, half, 1) * s2).astype(BF16)


def _qkv_rope(x, g, w, pos, freq):
    t, d = x.shape
    n = w.shape[1]
    tm = min(1024, t)
    tn = 4 * MXU_WIDTH
    return pl.pallas_call(
        _qkv_rope_kernel,
        grid=(t // tm, n // tn),
        in_specs=[pl.BlockSpec((tm, d), lambda i, j: (i, 0)),
                  pl.BlockSpec((1, d), lambda i, j: (0, 0)),
                  pl.BlockSpec((d, tn), lambda i, j: (0, j)),
                  pl.BlockSpec((tm, 1), lambda i, j: (i, 0)),
                  pl.BlockSpec((1, LANES), lambda i, j: (0, 0))],
        out_specs=pl.BlockSpec((tm, tn), lambda i, j: (i, j)),
        out_shape=jax.ShapeDtypeStruct((t, n), BF16),
        scratch_shapes=[pltpu.VMEM((tm, d), BF16), pltpu.VMEM((3, tm, LANES), F32)],
        compiler_params=_params(("parallel", "arbitrary")),
        name="qkv_rope",
    )(x, g.reshape(1, d), w, pos, freq)


def _attn_kernel(q_ref, k_ref, v_ref, lq1, lk1, lq2, lk2, sg_ref, o_ref, *, lambda_init, nsplit):
    lam = (jnp.exp(jnp.sum(lq1[...] * lk1[...], axis=-1, keepdims=True))
           - jnp.exp(jnp.sum(lq2[...] * lk2[...], axis=-1, keepdims=True)) + lambda_init)
    k = k_ref[...]
    v = v_ref[...]
    rows = q_ref.shape[0] // nsplit
    for part in range(nsplit):
        rs = slice(part * rows, (part + 1) * rows)
        q = q_ref[rs, :]
        s1 = _dot_nt(q[:, 0:DA_DH], k[:, 0:DA_DH])
        s2 = _dot_nt(q[:, DA_DH:], k[:, DA_DH:])
        e1 = jnp.exp2(s1 - jnp.max(s1, axis=-1, keepdims=True))
        e2 = jnp.exp2(s2 - jnp.max(s2, axis=-1, keepdims=True))
        r1 = 1.0 / jnp.sum(e1, axis=-1, keepdims=True)
        r2 = lam / jnp.sum(e2, axis=-1, keepdims=True)
        o = _dot(e1.astype(BF16), v) * r1 - _dot(e2.astype(BF16), v) * r2
        o_ref[rs, :] = (_rms(o, sg_ref[...]) * (1.0 - lambda_init)).astype(BF16)


def _diff_attention(qkvb, b, s, lq1, lk1, lq2, lk2, subln_g, lambda_init):
    t = b * s
    hw = 2 * DA_DH
    tq = min(1024, s)
    nq = s // tq

    def row(n):
        return pl.BlockSpec((1, n), lambda bb, h, i: (0, 0))

    return pl.pallas_call(
        functools.partial(_attn_kernel, lambda_init=lambda_init, nsplit=max(1, tq // 128)),
        grid=(b, DA_HEADS, nq),
        in_specs=[pl.BlockSpec((tq, hw), lambda bb, h, i: (bb * nq + i, h)),
                  pl.BlockSpec((s, hw), lambda bb, h, i: (bb, DA_HEADS + h)),
                  pl.BlockSpec((s, hw), lambda bb, h, i: (bb, 2 * DA_HEADS + h)),
                  row(DA_DH), row(DA_DH), row(DA_DH), row(DA_DH), row(hw)],
        out_specs=pl.BlockSpec((tq, hw), lambda bb, h, i: (bb * nq + i, h)),
        out_shape=jax.ShapeDtypeStruct((t, DA_HEADS * hw), BF16),
        compiler_params=_params(("parallel", "parallel", "arbitrary")),
        name="diff_attention",
    )(qkvb, qkvb, qkvb, lq1.reshape(1, -1), lk1.reshape(1, -1), lq2.reshape(1, -1),
      lk2.reshape(1, -1), subln_g.reshape(1, -1))


def _mm_res_kernel(a_ref, w_ref, x_ref, o_ref):
    o_ref[...] = x_ref[...] + _dot(a_ref[...], w_ref[...])


def _matmul_residual(a, w, x):
    t, d = x.shape
    kk = a.shape[1]
    tm = min(512, t)
    tn = 1024
    return pl.pallas_call(
        _mm_res_kernel,
        grid=(t // tm, d // tn),
        in_specs=[pl.BlockSpec((tm, kk), lambda i, j: (i, 0)),
                  pl.BlockSpec((kk, tn), lambda i, j: (0, j)),
                  pl.BlockSpec((tm, tn), lambda i, j: (i, j))],
        out_specs=pl.BlockSpec((tm, tn), lambda i, j: (i, j)),
        out_shape=jax.ShapeDtypeStruct((t, d), F32),
        compiler_params=_params(("parallel", "arbitrary")),
        name="matmul_residual",
    )(a, w, x)


def _odd_mixer(x, b, s, positions, norm_g, w_qkv, lq1, lk1, lq2, lk2, subln_g, w_o, lambda_init):
    t = b * s
    inv_freq = ROPE_THETA ** (-jnp.arange(0, ROPE_DIMS, 2, dtype=F32) / ROPE_DIMS)
    freq = jnp.concatenate([inv_freq, inv_freq, jnp.zeros((LANES - ROPE_DIMS,), F32)]).reshape(1, LANES)
    qkvb = _qkv_rope(x, norm_g, w_qkv, positions.reshape(t, 1), freq)
    attn = _diff_attention(qkvb, b, s, lq1, lk1, lq2, lk2, subln_g, lambda_init)
    return _matmul_residual(attn, w_o, x)


def kernel(x, positions, norm_ffn1, ffn1_wg, ffn1_wu, ffn1_wd, norm_mix, norm_ffn2, ffn2_wg, ffn2_wu, ffn2_wd, ev_w_in, ev_conv_w, ev_conv_b, ev_ln_g, ev_ln_b, ev_short_w, ev_a_log, ev_dt_bias, ev_onorm_g, ev_w_out, od_w_qkv, od_lq1, od_lk1, od_lq2, od_lk2, od_subln_g, od_w_o, final_norm):
    b, s, d = x.shape
    depth = norm_ffn1.shape[0]
    xf = x.reshape(b * s, d)
    for i in range(depth):
        xf = _ffn(xf, norm_ffn1[i], ffn1_wg, ffn1_wu, ffn1_wd, i)
        j = i // 2
        if i % 2 == 0:
            xf = _even_mixer(xf, b, s, norm_mix[i], _cast_bf16(ev_w_in, j, EVEN_IN_PAD), ev_conv_w[j],
                             ev_conv_b[j], ev_ln_g[j],
                             ev_ln_b[j], ev_short_w[j], ev_a_log[j], ev_dt_bias[j], ev_onorm_g[j],
                             _cast_bf16(ev_w_out, j))
        else:
            lambda_init = 0.8 - 0.6 * math.exp(-0.3 * i)
            xf = _odd_mixer(xf, b, s, positions, norm_mix[i], _cast_bf16(od_w_qkv, j), od_lq1[j], od_lk1[j],
                            od_lq2[j], od_lk2[j], od_subln_g[j], _cast_bf16(od_w_o, j), lambda_init)
        xf = _ffn(xf, norm_ffn2[i], ffn2_wg, ffn2_wu, ffn2_wd, i,
                  final_g=final_norm if i == depth - 1 else None)
    return xf.reshape(b, s, d)
```
